```python
import jax, jax.numpy as jnp
from jax import lax
import numpy as np

D_MODEL = 2048
BATCH = 1
SEQ = 16384
DEPTH = 2

HEAD_DIM = 128
CONV_CH = 512
CONV_WIDTH = 31
CONV_HALF = CONV_WIDTH // 2
B_Q_HEADS = 8
B_KV_HEADS = 2
B_GROUP = B_Q_HEADS // B_KV_HEADS
B_HALF_WINDOW = 128
C_PATTERNS = ((128, 1), (512, 4), (2048, 16))
C_GROUPS = len(C_PATTERNS)
C_HEADS_PER_GROUP = 4
C_HEADS = C_GROUPS * C_HEADS_PER_GROUP
N_BRANCHES = 3
N_ATTN_HEADS = B_Q_HEADS + C_HEADS
ALIBI_MAX_EXP = 8.0
D_FF = -(-8 * D_MODEL // (3 * 256)) * 256
EPS = 1e-6
NEG = -1e30

A_IN_W = 2 * CONV_CH
BQ_W = B_Q_HEADS * HEAD_DIM
BKV_W = B_KV_HEADS * HEAD_DIM
CQKV_W = C_HEADS * HEAD_DIM
GATE_W = N_BRANCHES * D_MODEL
IN_SPLITS = (A_IN_W, BQ_W, BKV_W, BKV_W, CQKV_W, CQKV_W, CQKV_W, GATE_W)
IN_WIDTH = sum(IN_SPLITS)
IN_SPLIT_IDX = tuple(int(v) for v in np.cumsum(IN_SPLITS)[:-1])
B_OUT_W = B_Q_HEADS * HEAD_DIM
C_OUT_W = C_HEADS_PER_GROUP * HEAD_DIM

kernel_name = "hybrid_conv_swa_dilated_gated_encoder"


def rmsnorm(x, g):
    xf = x.astype(jnp.float32)
    y = xf * lax.rsqrt(jnp.mean(xf * xf, axis=-1, keepdims=True) + EPS)
    return (y * g.astype(jnp.float32)).astype(x.dtype)


def layernorm(x, g, b):
    xf = x.astype(jnp.float32)
    mu = jnp.mean(xf, axis=-1, keepdims=True)
    var = jnp.mean(jnp.square(xf - mu), axis=-1, keepdims=True)
    y = (xf - mu) * lax.rsqrt(var + EPS)
    return (y * g.astype(jnp.float32) + b.astype(jnp.float32)).astype(x.dtype)


def alibi_slopes():
    h = jnp.arange(1, N_ATTN_HEADS + 1, dtype=jnp.float32)
    return jnp.exp2(-ALIBI_MAX_EXP * h / N_ATTN_HEADS)


def banded_attention(q, k, v, slopes, half, step, n_valid, sink=None):
    n, length, hk, g, dh = q.shape
    blk = half
    nb = length // blk
    qb = q.reshape(n, nb, blk, hk, g, dh)

    def windows(t):
        pad = jnp.zeros((n, blk, hk, dh), t.dtype)
        tp = jnp.concatenate([pad, t, pad], axis=1).reshape(n, nb + 2, blk, hk, dh)
        return jnp.concatenate([tp[:, :-2], tp[:, 1:-1], tp[:, 2:]], axis=2)

    kw, vw = windows(k), windows(v)
    s = jnp.einsum("nbqhgd,nbkhd->nbhgqk", qb, kw).astype(jnp.float32) * (dh ** -0.5)
    qi = jnp.arange(blk)
    kj = jnp.arange(3 * blk)
    rel = kj[None, :] - blk - qi[:, None]
    k_pos = (jnp.arange(nb)[:, None] - 1) * blk + kj[None, :]
    valid = (jnp.abs(rel) <= half)[None] & ((k_pos >= 0) & (k_pos < n_valid))[:, None, :]
    dist = (step * jnp.abs(rel)).astype(jnp.float32)
    bias = -slopes.astype(jnp.float32)[:, :, None, None] * dist[None, None]
    s = jnp.where(valid[None, :, None, None], s + bias, NEG)
    m = jnp.max(s, axis=-1, keepdims=True)
    if sink is not None:
        sink_b = sink.astype(jnp.float32)[:, :, None, None]
        m = jnp.maximum(m, sink_b)
    p = jnp.exp(s - m)
    denom = jnp.sum(p, axis=-1, keepdims=True)
    if sink is not None:
        denom = denom + jnp.exp(sink_b - m)
    o = jnp.einsum("nbhgqk,nbkhd->nbqhgd", p.astype(v.dtype), vw).astype(jnp.float32)
    o = o / jnp.transpose(denom, (0, 1, 4, 2, 3, 5))
    lse = jnp.transpose((m + jnp.log(denom))[..., 0], (0, 1, 4, 2, 3)).reshape(n, length, hk, g)
    return o.reshape(n, length, hk, g, dh).astype(q.dtype), lse


def conv_module(u, conv_w, conv_b, cnorm_g, cnorm_b):
    a, gt = jnp.split(u, 2, axis=-1)
    z = a * jax.nn.sigmoid(gt)
    z = lax.conv_general_dilated(
        z, conv_w[:, None, :], window_strides=(1,), padding=[(CONV_HALF, CONV_HALF)],
        dimension_numbers=("NWC", "WIO", "NWC"), feature_group_count=CONV_CH)
    z = z + conv_b
    return jax.nn.silu(layernorm(z, cnorm_g, cnorm_b))


def windowed_gqa(q, k, v, sink, slopes):
    b, s, _ = q.shape
    q = q.reshape(b, s, B_KV_HEADS, B_GROUP, HEAD_DIM)
    k = k.reshape(b, s, B_KV_HEADS, HEAD_DIM)
    v = v.reshape(b, s, B_KV_HEADS, HEAD_DIM)
    o, _ = banded_attention(q, k, v, slopes.reshape(B_KV_HEADS, B_GROUP), B_HALF_WINDOW, 1, s,
                            sink=sink.reshape(B_KV_HEADS, B_GROUP))
    return o.reshape(b, s, B_OUT_W)


def to_lattice(t, r, length, padded):
    b, s, h, dh = t.shape
    t = t.reshape(b, length, r, h, dh).transpose(0, 2, 1, 3, 4).reshape(b * r, length, h, dh)
    return jnp.pad(t, ((0, 0), (0, padded - length), (0, 0), (0, 0)))


def dilated_attention(q, k, v, slopes):
    b, s, _ = q.shape
    q = q.reshape(b, s, C_HEADS, HEAD_DIM)
    k = k.reshape(b, s, C_HEADS, HEAD_DIM)
    v = v.reshape(b, s, C_HEADS, HEAD_DIM)
    outs, lses = [], []
    for gi, (w, r) in enumerate(C_PATTERNS):
        half = w // (2 * r)
        length = s // r
        padded = -(-length // half) * half
        hs = slice(gi * C_HEADS_PER_GROUP, (gi + 1) * C_HEADS_PER_GROUP)
        o, lse = banded_attention(
            to_lattice(q[:, :, hs], r, length, padded)[:, :, :, None],
            to_lattice(k[:, :, hs], r, length, padded),
            to_lattice(v[:, :, hs], r, length, padded),
            slopes[hs][:, None], half, r, length)
        o = o[:, :length, :, 0].reshape(b, r, length, C_HEADS_PER_GROUP, HEAD_DIM)
        outs.append(o.transpose(0, 2, 1, 3, 4).reshape(b, s, C_HEADS_PER_GROUP, HEAD_DIM))
        lse = lse[:, :length, :, 0].reshape(b, r, length, C_HEADS_PER_GROUP)
        lses.append(lse.transpose(0, 2, 1, 3).reshape(b, s, C_HEADS_PER_GROUP))
    alpha = jax.nn.softmax(jnp.stack(lses), axis=0)
    out = jnp.einsum("gbsh,gbshd->bshd", alpha.astype(q.dtype), jnp.stack(outs))
    return out.reshape(b, s, C_OUT_W)


def hybrid_layer(x, ln1_g, w_in, conv_w, conv_b, cnorm_g, cnorm_b, w_a, sink, w_b, w_c, w_o,
                 ln2_g, w_ffn_in, w_ffn_out):
    b, s, d = x.shape
    h = rmsnorm(x, ln1_g)
    proj = h @ w_in
    u_a, q_b, k_b, v_b, q_c, k_c, v_c, gate_logits = jnp.split(proj, IN_SPLIT_IDX, axis=-1)
    slopes = alibi_slopes()
    y_a = conv_module(u_a, conv_w, conv_b, cnorm_g, cnorm_b) @ w_a
    y_b = windowed_gqa(q_b, k_b, v_b, sink, slopes[:B_Q_HEADS]) @ w_b
    y_c = dilated_attention(q_c, k_c, v_c, slopes[B_Q_HEADS:]) @ w_c
    gates = jax.nn.sigmoid(gate_logits.astype(jnp.float32)).astype(x.dtype).reshape(b, s, N_BRANCHES, d)
    mixed = gates[:, :, 0] * y_a + gates[:, :, 1] * y_b + gates[:, :, 2] * y_c
    x = x + mixed @ w_o
    h2 = rmsnorm(x, ln2_g)
    g_ff, u_ff = jnp.split(h2 @ w_ffn_in, 2, axis=-1)
    return x + (jax.nn.silu(g_ff) * u_ff) @ w_ffn_out


def setup_inputs(seed: int = 0) -> dict:
    key = jax.random.key(seed)
    ks = jax.random.split(key, 17)
    f32 = jnp.float32

    def nrm(k, shape, scale):
        return jax.random.normal(k, shape, f32) * scale

    return {
        "x": nrm(ks[0], (BATCH, SEQ, D_MODEL), 1.0),
        "ln1_g": 1.0 + nrm(ks[1], (DEPTH, D_MODEL), 0.02),
        "w_in": nrm(ks[2], (DEPTH, D_MODEL, IN_WIDTH), D_MODEL ** -0.5),
        "conv_w": nrm(ks[3], (DEPTH, CONV_WIDTH, CONV_CH), CONV_WIDTH ** -0.5),
        "conv_b": nrm(ks[4], (DEPTH, CONV_CH), 0.02),
        "cnorm_g": 1.0 + nrm(ks[5], (DEPTH, CONV_CH), 0.02),
        "cnorm_b": nrm(ks[6], (DEPTH, CONV_CH), 0.02),
        "w_a": nrm(ks[7], (DEPTH, CONV_CH, D_MODEL), CONV_CH ** -0.5),
        "sink": nrm(ks[8], (DEPTH, B_Q_HEADS), 0.5),
        "w_b": nrm(ks[9], (DEPTH, B_OUT_W, D_MODEL), B_OUT_W ** -0.5),
        "w_c": nrm(ks[10], (DEPTH, C_OUT_W, D_MODEL), C_OUT_W ** -0.5),
        "w_o": nrm(ks[11], (DEPTH, D_MODEL, D_MODEL), D_MODEL ** -0.5),
        "ln2_g": 1.0 + nrm(ks[12], (DEPTH, D_MODEL), 0.02),
        "w_ffn_in": nrm(ks[13], (DEPTH, D_MODEL, 2 * D_FF), D_MODEL ** -0.5),
        "w_ffn_out": nrm(ks[14], (DEPTH, D_FF, D_MODEL), D_FF ** -0.5),
        "lnf_g": 1.0 + nrm(ks[15], (D_MODEL,), 0.02),
    }


def reference(x, ln1_g, w_in, conv_w, conv_b, cnorm_g, cnorm_b, w_a, sink, w_b, w_c, w_o,
              ln2_g, w_ffn_in, w_ffn_out, lnf_g):
    for l in range(DEPTH):
        x = hybrid_layer(x, ln1_g[l], w_in[l], conv_w[l], conv_b[l], cnorm_g[l], cnorm_b[l],
                         w_a[l], sink[l], w_b[l], w_c[l], w_o[l], ln2_g[l], w_ffn_in[l],
                         w_ffn_out[l])
    return rmsnorm(x, lnf_g)
```

```python
import functools

import jax
import jax.numpy as jnp
from jax import lax
from jax.experimental import pallas as pl
from jax.experimental.pallas import tpu as pltpu

F32 = jnp.float32
BF16 = jnp.bfloat16

D_MODEL = 2048
HEAD_DIM = 128
CONV_CH = 512
CONV_WIDTH = 31
CONV_HALF = CONV_WIDTH // 2
B_Q_HEADS = 8
B_KV_HEADS = 2
B_GROUP = B_Q_HEADS // B_KV_HEADS
B_HALF_WINDOW = 128
C_PATTERNS = ((128, 1), (512, 4), (2048, 16))
C_HEADS_PER_GROUP = 4
C_HALF = 64
N_BRANCHES = 3
N_ATTN_HEADS = B_Q_HEADS + len(C_PATTERNS) * C_HEADS_PER_GROUP
ALIBI_MAX_EXP = 8.0
D_FF = -(-8 * D_MODEL // (3 * 256)) * 256
EPS = 1e-6
NEG = -1e30

LANES = 128
COL = 512
PLAIN_COLS = 5 * COL
QC_BLOCK0, KC_BLOCK0, VC_BLOCK0 = 5, 8, 11
GATE_BLOCK0 = 14
C_TOKENS = 1024

VMEM_LIMIT = 56 * 1024 * 1024


def _params(n_axes):
    return pltpu.CompilerParams(
        dimension_semantics=("arbitrary",) * n_axes, vmem_limit_bytes=VMEM_LIMIT)


def _rmsnorm_kernel(x_ref, g_ref, o_ref):
    x = x_ref[...]
    ms = jnp.mean(x * x, axis=-1, keepdims=True)
    o_ref[...] = (x * lax.rsqrt(ms + EPS) * g_ref[...]).astype(o_ref.dtype)


def rmsnorm(x, g, out_dtype, tm=512):
    s, d = x.shape
    return pl.pallas_call(
        _rmsnorm_kernel,
        grid=(s // tm,),
        in_specs=[pl.BlockSpec((tm, d), lambda i: (i, 0)),
                  pl.BlockSpec((1, d), lambda i: (0, 0))],
        out_specs=pl.BlockSpec((tm, d), lambda i: (i, 0)),
        out_shape=jax.ShapeDtypeStruct((s, d), out_dtype),
        compiler_params=_params(1),
        name="rmsnorm",
    )(x, g.reshape(1, d))


def _mm_kernel(a_ref, w_ref, o_ref):
    o_ref[...] = jnp.dot(a_ref[...], w_ref[...],
                         preferred_element_type=F32).astype(o_ref.dtype)


def proj_plain(h, w16, tm=1024, tn=COL):
    s, d = h.shape
    return pl.pallas_call(
        _mm_kernel,
        grid=(s // tm, PLAIN_COLS // tn),
        in_specs=[pl.BlockSpec((tm, d), lambda i, j: (i, 0)),
                  pl.BlockSpec((d, tn), lambda i, j: (0, j))],
        out_specs=pl.BlockSpec((tm, tn), lambda i, j: (i, j)),
        out_shape=jax.ShapeDtypeStruct((s, PLAIN_COLS), BF16),
        compiler_params=_params(2),
        name="proj_plain",
    )(h, w16)


def _mm_lattice_kernel(a_ref, w_ref, o_ref, acc_ref, *, r):
    acc = jnp.dot(a_ref[...], w_ref[...], preferred_element_type=F32)
    tm, tn = acc.shape
    if r == 1:
        o_ref[0] = acc.astype(o_ref.dtype)
        return
    rows = tm // r
    for s in range(tn // LANES):
        acc_ref[s] = acc[:, s * LANES:(s + 1) * LANES]
    for c in range(r):
        for s in range(tn // LANES):
            o_ref[c, :, s * LANES:(s + 1) * LANES] = (
                acc_ref[s, pl.ds(c, rows, stride=r), :].astype(o_ref.dtype))


def proj_lattice(h, w16, group, r, tm=1024):
    s, d = h.shape
    return pl.pallas_call(
        functools.partial(_mm_lattice_kernel, r=r),
        grid=(s // tm, 3),
        in_specs=[pl.BlockSpec((tm, d), lambda i, j: (i, 0)),
                  pl.BlockSpec((d, COL), lambda i, j: (0, QC_BLOCK0 + group + 3 * j))],
        out_specs=pl.BlockSpec((r, tm // r, COL), lambda i, j: (0, i, j)),
        out_shape=jax.ShapeDtypeStruct((r, s // r, 3 * COL), BF16),
        scratch_shapes=[pltpu.VMEM((COL // LANES, tm, LANES), F32)],
        compiler_params=_params(2),
        name=f"proj_lattice_r{r}",
    )(h, w16)


def _merge_kernel(h_ref, ca_ref, ob_ref, oc_ref, wg0_ref, wg1_ref, wg2_ref,
                  wa_ref, wb_ref, wc_ref, o_ref):
    h = h_ref[...]

    def gate(wg_ref):
        return jax.nn.sigmoid(jnp.dot(h, wg_ref[...], preferred_element_type=F32))

    def branch(y_ref, w_ref):
        return jnp.dot(y_ref[...], w_ref[...], preferred_element_type=F32)

    mixed = (gate(wg0_ref) * branch(ca_ref, wa_ref)
             + gate(wg1_ref) * branch(ob_ref, wb_ref)
             + gate(wg2_ref) * branch(oc_ref, wc_ref))
    o_ref[...] = mixed.astype(o_ref.dtype)


def gated_merge(h, ca, ob, oc, w_in16, wa16, wb16, wc16, tm=1024, tn=COL):
    s, d = h.shape
    nj = d // tn

    def gate_spec(b):
        return pl.BlockSpec((d, tn), lambda i, j: (0, GATE_BLOCK0 + b * nj + j))

    def lhs_spec(width):
        return pl.BlockSpec((tm, width), lambda i, j: (i, 0))

    def w_spec(width):
        return pl.BlockSpec((width, tn), lambda i, j: (0, j))

    return pl.pallas_call(
        _merge_kernel,
        grid=(s // tm, nj),
        in_specs=[lhs_spec(d), lhs_spec(ca.shape[1]), lhs_spec(ob.shape[1]), lhs_spec(oc.shape[1]),
                  gate_spec(0), gate_spec(1), gate_spec(2),
                  w_spec(ca.shape[1]), w_spec(ob.shape[1]), w_spec(oc.shape[1])],
        out_specs=pl.BlockSpec((tm, tn), lambda i, j: (i, j)),
        out_shape=jax.ShapeDtypeStruct((s, d), BF16),
        compiler_params=_params(2),
        name="gated_merge",
    )(h, ca, ob, oc, w_in16, w_in16, w_in16, wa16, wb16, wc16)


def _mm_residual_kernel(a_ref, w_ref, x_ref, o_ref):
    o_ref[...] = x_ref[...] + jnp.dot(a_ref[...], w_ref[...], preferred_element_type=F32)


def matmul_residual(a, w16, x, tm=1024, tn=COL):
    s, k = a.shape
    d = w16.shape[1]
    return pl.pallas_call(
        _mm_residual_kernel,
        grid=(s // tm, d // tn),
        in_specs=[pl.BlockSpec((tm, k), lambda i, j: (i, 0)),
                  pl.BlockSpec((k, tn), lambda i, j: (0, j)),
                  pl.BlockSpec((tm, tn), lambda i, j: (i, j))],
        out_specs=pl.BlockSpec((tm, tn), lambda i, j: (i, j)),
        out_shape=jax.ShapeDtypeStruct((s, d), F32),
        compiler_params=_params(2),
        name="matmul_residual",
    )(a, w16, x)


def _swiglu_kernel(h_ref, wg_ref, wu_ref, o_ref):
    h = h_ref[...]
    g = jnp.dot(h, wg_ref[...], preferred_element_type=F32)
    u = jnp.dot(h, wu_ref[...], preferred_element_type=F32)
    o_ref[...] = (jax.nn.silu(g) * u).astype(o_ref.dtype)


def swiglu_in(h, w16, tm=1024, tn=COL):
    s, d = h.shape
    nj = D_FF // tn
    return pl.pallas_call(
        _swiglu_kernel,
        grid=(s // tm, nj),
        in_specs=[pl.BlockSpec((tm, d), lambda i, j: (i, 0)),
                  pl.BlockSpec((d, tn), lambda i, j: (0, j)),
                  pl.BlockSpec((d, tn), lambda i, j: (0, nj + j))],
        out_specs=pl.BlockSpec((tm, tn), lambda i, j: (i, j)),
        out_shape=jax.ShapeDtypeStruct((s, D_FF), BF16),
        compiler_params=_params(2),
        name="swiglu_in",
    )(h, w16, w16)


def _conv_kernel(up_ref, uc_ref, un_ref, w_ref, b_ref, g_ref, beta_ref, o_ref, z_ref, *, tm, halo, chunk):
    i = pl.program_id(0)
    n = pl.num_programs(0)

    def glu(u):
        u = u.astype(F32)
        return u[:, :CONV_CH] * jax.nn.sigmoid(u[:, CONV_CH:])

    z_ref[0:halo] = jnp.where(i > 0, glu(up_ref[...]), 0.0)
    z_ref[halo:halo + tm] = glu(uc_ref[...])
    z_ref[halo + tm:2 * halo + tm] = jnp.where(i < n - 1, glu(un_ref[...]), 0.0)

    for rc in range(tm // chunk):
        base = rc * chunk + halo - CONV_HALF
        acc = jnp.zeros((chunk, CONV_CH), F32)
        for k in range(CONV_WIDTH):
            acc = acc + z_ref[base + k:base + k + chunk, :] * w_ref[k:k + 1, :]
        z = acc + b_ref[...]
        mu = jnp.mean(z, axis=-1, keepdims=True)
        zc = z - mu
        var = jnp.mean(zc * zc, axis=-1, keepdims=True)
        y = zc * lax.rsqrt(var + EPS) * g_ref[...] + beta_ref[...]
        o_ref[rc * chunk:(rc + 1) * chunk, :] = jax.nn.silu(y).astype(o_ref.dtype)


def conv_module(plain, conv_w, conv_b, cnorm_g, cnorm_b, tm=512, halo=16, chunk=64):
    s = plain.shape[0]
    nh = tm // halo
    width = 2 * CONV_CH
    row = lambda v: v.reshape(1, CONV_CH)
    vec_spec = pl.BlockSpec((1, CONV_CH), lambda i: (0, 0))
    return pl.pallas_call(
        functools.partial(_conv_kernel, tm=tm, halo=halo, chunk=chunk),
        grid=(s // tm,),
        in_specs=[pl.BlockSpec((halo, width), lambda i: (jnp.maximum(i * nh - 1, 0), 0)),
                  pl.BlockSpec((tm, width), lambda i: (i, 0)),
                  pl.BlockSpec((halo, width), lambda i: (jnp.minimum((i + 1) * nh, s // halo - 1), 0)),
                  pl.BlockSpec((CONV_WIDTH, CONV_CH), lambda i: (0, 0)),
                  vec_spec, vec_spec, vec_spec],
        out_specs=pl.BlockSpec((tm, CONV_CH), lambda i: (i, 0)),
        out_shape=jax.ShapeDtypeStruct((s, CONV_CH), BF16),
        scratch_shapes=[pltpu.VMEM((tm + 2 * halo, CONV_CH), F32)],
        compiler_params=_params(1),
        name="conv_module",
    )(plain, plain, plain, conv_w, row(conv_b), row(cnorm_g), row(cnorm_b))


def _band_tables(half, step):
    qi = lax.broadcasted_iota(jnp.int32, (half, 3 * half), 0)
    kj = lax.broadcasted_iota(jnp.int32, (half, 3 * half), 1)
    rel = jnp.abs(kj - half - qi)
    return rel <= half, (step * rel).astype(F32), kj


def _band_block(q, k, v, bias, valid, sink):
    sc = lax.dot_general(q, k, (((1,), (1,)), ((), ())), preferred_element_type=F32)
    sc = jnp.where(valid, sc * (HEAD_DIM ** -0.5) + bias, NEG)
    m = jnp.max(sc, axis=-1, keepdims=True)
    if sink is not None:
        m = jnp.maximum(m, sink)
    p = jnp.exp(sc - m)
    denom = jnp.sum(p, axis=-1, keepdims=True)
    if sink is not None:
        denom = denom + jnp.exp(sink - m)
    o = jnp.dot(p.astype(v.dtype), v, preferred_element_type=F32) / denom
    return o, m + jnp.log(denom)


def _edge_valid(band, kj, half, blk, n_blocks, first, last):
    valid = band
    if first:
        valid = valid & (kj >= jnp.where(blk > 0, 0, half))
    if last:
        valid = valid & (kj < jnp.where(blk < n_blocks - 1, 3 * half, 2 * half))
    return valid


def _swa_kernel(slopes_ref, sink_ref, q_ref, kp_ref, kc_ref, kn_ref, vp_ref, vc_ref, vn_ref, o_ref,
                *, tq, half, n_blocks):
    i = pl.program_id(0)
    hk = pl.program_id(1)
    nsub = tq // half
    kcat = jnp.concatenate([kp_ref[...], kc_ref[...], kn_ref[...]], axis=0)
    vcat = jnp.concatenate([vp_ref[...], vc_ref[...], vn_ref[...]], axis=0)
    band, dist, kj = _band_tables(half, 1)
    for g in range(B_GROUP):
        head = hk * B_GROUP + g
        bias = -slopes_ref[head] * dist
        sink = sink_ref[head]
        cols = slice(g * HEAD_DIM, (g + 1) * HEAD_DIM)
        for s in range(nsub):
            valid = _edge_valid(band, kj, half, i * nsub + s, n_blocks, s == 0, s == nsub - 1)
            o, _ = _band_block(q_ref[s * half:(s + 1) * half, cols],
                               kcat[s * half:(s + 3) * half], vcat[s * half:(s + 3) * half],
                               bias, valid, sink)
            o_ref[s * half:(s + 1) * half, cols] = o.astype(o_ref.dtype)


def windowed_gqa(plain, slopes, sink, tq=512):
    s = plain.shape[0]
    half = B_HALF_WINDOW
    nh = tq // half
    qw = B_GROUP * HEAD_DIM
    q_blk0 = 2 * CONV_CH // qw
    k_blk0 = (2 * CONV_CH + B_Q_HEADS * HEAD_DIM) // HEAD_DIM
    v_blk0 = k_blk0 + B_KV_HEADS

    def halo_specs(blk0):
        return [pl.BlockSpec((half, HEAD_DIM), lambda i, h: (jnp.maximum(i * nh - 1, 0), blk0 + h)),
                pl.BlockSpec((tq, HEAD_DIM), lambda i, h: (i, blk0 + h)),
                pl.BlockSpec((half, HEAD_DIM),
                             lambda i, h: (jnp.minimum((i + 1) * nh, s // half - 1), blk0 + h))]

    smem = pl.BlockSpec(memory_space=pltpu.SMEM)
    return pl.pallas_call(
        functools.partial(_swa_kernel, tq=tq, half=half, n_blocks=s // half),
        grid=(s // tq, B_KV_HEADS),
        in_specs=[smem, smem, pl.BlockSpec((tq, qw), lambda i, h: (i, q_blk0 + h))]
        + halo_specs(k_blk0) + halo_specs(v_blk0),
        out_specs=pl.BlockSpec((tq, qw), lambda i, h: (i, h)),
        out_shape=jax.ShapeDtypeStruct((s, B_Q_HEADS * HEAD_DIM), BF16),
        compiler_params=_params(2),
        name="windowed_gqa",
    )(slopes, sink, plain, plain, plain, plain, plain, plain, plain)


def _dilated_kernel(slopes_ref, *refs, tokens, half, seq_len):
    o_ref, o_scr, l_scr = refs[-3:]
    i = pl.program_id(0)
    h = pl.program_id(1)
    for gi, (_, r) in enumerate(C_PATTERNS):
        q_ref, kp_ref, kc_ref, kn_ref, vp_ref, vc_ref, vn_ref = refs[7 * gi:7 * gi + 7]
        rows = tokens // r
        nsub = rows // half
        n_blocks = seq_len // r // half
        band, dist, kj = _band_tables(half, r)
        bias = -slopes_ref[B_Q_HEADS + gi * C_HEADS_PER_GROUP + h] * dist
        for c in range(r):
            kcat = jnp.concatenate([kp_ref[c], kc_ref[c], kn_ref[c]], axis=0)
            vcat = jnp.concatenate([vp_ref[c], vc_ref[c], vn_ref[c]], axis=0)
            for s in range(nsub):
                valid = _edge_valid(band, kj, half, i * nsub + s, n_blocks, s == 0, s == nsub - 1)
                o, lse = _band_block(q_ref[c, s * half:(s + 1) * half, :],
                                     kcat[s * half:(s + 3) * half], vcat[s * half:(s + 3) * half],
                                     bias, valid, None)
                dst = pl.ds(s * half * r + c, half, stride=r) if r > 1 else pl.ds(s * half, half)
                o_scr[gi, dst, :] = o
                l_scr[gi, dst, :] = jnp.broadcast_to(lse, (half, HEAD_DIM))
    l0, l1, l2 = l_scr[0], l_scr[1], l_scr[2]
    m = jnp.maximum(jnp.maximum(l0, l1), l2)
    e0, e1, e2 = jnp.exp(l0 - m), jnp.exp(l1 - m), jnp.exp(l2 - m)
    out = (e0 * o_scr[0] + e1 * o_scr[1] + e2 * o_scr[2]) / (e0 + e1 + e2)
    o_ref[...] = out.astype(o_ref.dtype)


def dilated_attention(lattices, slopes, seq_len, tokens=C_TOKENS, half=C_HALF):
    nq = C_HEADS_PER_GROUP
    in_specs = [pl.BlockSpec(memory_space=pltpu.SMEM)]
    args = [slopes]
    for (_, r), lat in zip(C_PATTERNS, lattices):
        rows = tokens // r
        nh = rows // half
        last = seq_len // r // half - 1

        def specs(blk0, r=r, rows=rows, nh=nh, last=last):
            return [pl.BlockSpec((r, half, HEAD_DIM), lambda i, h: (0, jnp.maximum(i * nh - 1, 0), blk0 + h)),
                    pl.BlockSpec((r, rows, HEAD_DIM), lambda i, h: (0, i, blk0 + h)),
                    pl.BlockSpec((r, half, HEAD_DIM), lambda i, h: (0, jnp.minimum((i + 1) * nh, last), blk0 + h))]

        in_specs += [pl.BlockSpec((r, rows, HEAD_DIM), lambda i, h: (0, i, h))] + specs(nq) + specs(2 * nq)
        args += [lat] * 7
    return pl.pallas_call(
        functools.partial(_dilated_kernel, tokens=tokens, half=half, seq_len=seq_len),
        grid=(seq_len // tokens, nq),
        in_specs=in_specs,
        out_specs=pl.BlockSpec((tokens, HEAD_DIM), lambda i, h: (i, h)),
        out_shape=jax.ShapeDtypeStruct((seq_len, nq * HEAD_DIM), BF16),
        scratch_shapes=[pltpu.VMEM((len(C_PATTERNS), tokens, HEAD_DIM), F32),
                        pltpu.VMEM((len(C_PATTERNS), tokens, HEAD_DIM), F32)],
        compiler_params=_params(2),
        name="dilated_attention",
    )(*args)


def _alibi_slopes():
    h = jnp.arange(1, N_ATTN_HEADS + 1, dtype=F32)
    return jnp.exp2(-ALIBI_MAX_EXP * h / N_ATTN_HEADS)


def _layer(x, slopes, ln1_g, w_in, conv_w, conv_b, cnorm_g, cnorm_b, w_a, sink, w_b, w_c, w_o,
           ln2_g, w_ffn_in, w_ffn_out):
    s = x.shape[0]
    w_in16 = w_in.astype(BF16)
    h = rmsnorm(x, ln1_g, BF16)
    plain = proj_plain(h, w_in16)
    lattices = [proj_lattice(h, w_in16, gi, r) for gi, (_, r) in enumerate(C_PATTERNS)]
    ca = conv_module(plain, conv_w, conv_b, cnorm_g, cnorm_b)
    ob = windowed_gqa(plain, slopes, sink)
    oc = dilated_attention(lattices, slopes, s)
    mixed = gated_merge(h, ca, ob, oc, w_in16, w_a.astype(BF16), w_b.astype(BF16), w_c.astype(BF16))
    x = matmul_residual(mixed, w_o.astype(BF16), x)
    h2 = rmsnorm(x, ln2_g, BF16)
    act = swiglu_in(h2, w_ffn_in.astype(BF16))
    return matmul_residual(act, w_ffn_out.astype(BF16), x)


def kernel(x, ln1_g, w_in, conv_w, conv_b, cnorm_g, cnorm_b, w_a, sink, w_b, w_c, w_o, ln2_g,
           w_ffn_in, w_ffn_out, lnf_g):
    b, s, d = x.shape
    slopes = _alibi_slopes()
    outs = []
    for xb in jnp.split(x, b, axis=0):
        xb = xb.reshape(s, d)
        for l in range(ln1_g.shape[0]):
            xb = _layer(xb, slopes, ln1_g[l], w_in[l], conv_w[l], conv_b[l], cnorm_g[l], cnorm_b[l],
                        w_a[l], sink[l], w_b[l], w_c[l], w_o[l], ln2_g[l], w_ffn_in[l], w_ffn_out[l])
        outs.append(rmsnorm(xb, lnf_g, x.dtype).reshape(1, s, d))
    return outs[0] if b == 1 else jnp.concatenate(outs, axis=0)
```

```python
import functools

import jax
import jax.numpy as jnp
from jax import lax
from jax.experimental import pallas as pl
from jax.experimental.pallas import tpu as pltpu

F32 = jnp.float32
BF16 = jnp.bfloat16

D_MODEL = 2048
HEAD_DIM = 128
CONV_CH = 512
CONV_WIDTH = 31
CONV_HALF = CONV_WIDTH // 2
B_Q_HEADS = 8
B_KV_HEADS = 2
B_GROUP = B_Q_HEADS // B_KV_HEADS
B_HALF_WINDOW = 128
C_PATTERNS = ((128, 1), (512, 4), (2048, 16))
C_HEADS_PER_GROUP = 4
C_HALF = 64
N_BRANCHES = 3
N_ATTN_HEADS = B_Q_HEADS + len(C_PATTERNS) * C_HEADS_PER_GROUP
ALIBI_MAX_EXP = 8.0
D_FF = -(-8 * D_MODEL // (3 * 256)) * 256
EPS = 1e-6
NEG = -1e30

LANES = 128
COL = 512
PLAIN_COLS = 5 * COL
QC_BLOCK0, KC_BLOCK0, VC_BLOCK0 = 5, 8, 11
GATE_BLOCK0 = 14
C_TOKENS = 2048
C_BLOCK_Q = 128

VMEM_LIMIT = 56 * 1024 * 1024


def _params(n_axes):
    return pltpu.CompilerParams(
        dimension_semantics=("arbitrary",) * n_axes, vmem_limit_bytes=VMEM_LIMIT)


def _rmsnorm_kernel(x_ref, g_ref, o_ref):
    x = x_ref[...]
    ms = jnp.mean(x * x, axis=-1, keepdims=True)
    o_ref[...] = (x * lax.rsqrt(ms + EPS) * g_ref[...]).astype(o_ref.dtype)


def rmsnorm(x, g, out_dtype, tm=512):
    s, d = x.shape
    return pl.pallas_call(
        _rmsnorm_kernel,
        grid=(s // tm,),
        in_specs=[pl.BlockSpec((tm, d), lambda i: (i, 0)),
                  pl.BlockSpec((1, d), lambda i: (0, 0))],
        out_specs=pl.BlockSpec((tm, d), lambda i: (i, 0)),
        out_shape=jax.ShapeDtypeStruct((s, d), out_dtype),
        compiler_params=_params(1),
        name="rmsnorm",
    )(x, g.reshape(1, d))


def _mm_kernel(a_ref, w_ref, o_ref):
    o_ref[...] = jnp.dot(a_ref[...], w_ref[...],
                         preferred_element_type=F32).astype(o_ref.dtype)


def proj_plain(h, w16, tm=1024, tn=COL):
    s, d = h.shape
    return pl.pallas_call(
        _mm_kernel,
        grid=(s // tm, PLAIN_COLS // tn),
        in_specs=[pl.BlockSpec((tm, d), lambda i, j: (i, 0)),
                  pl.BlockSpec((d, tn), lambda i, j: (0, j))],
        out_specs=pl.BlockSpec((tm, tn), lambda i, j: (i, j)),
        out_shape=jax.ShapeDtypeStruct((s, PLAIN_COLS), BF16),
        compiler_params=_params(2),
        name="proj_plain",
    )(h, w16)


def _mm_lattice_kernel(a_ref, w_ref, o_ref, acc_ref, *, r):
    acc = jnp.dot(a_ref[...], w_ref[...], preferred_element_type=F32)
    tm, tn = acc.shape
    if r == 1:
        o_ref[0] = acc.astype(o_ref.dtype)
        return
    rows = tm // r
    for s in range(tn // LANES):
        acc_ref[s] = acc[:, s * LANES:(s + 1) * LANES]
    for c in range(r):
        for s in range(tn // LANES):
            o_ref[c, :, s * LANES:(s + 1) * LANES] = (
                acc_ref[s, pl.ds(c, rows, stride=r), :].astype(o_ref.dtype))


def proj_lattice(h, w16, group, r, tm=1024):
    s, d = h.shape
    return pl.pallas_call(
        functools.partial(_mm_lattice_kernel, r=r),
        grid=(s // tm, 3),
        in_specs=[pl.BlockSpec((tm, d), lambda i, j: (i, 0)),
                  pl.BlockSpec((d, COL), lambda i, j: (0, QC_BLOCK0 + group + 3 * j))],
        out_specs=pl.BlockSpec((r, tm // r, COL), lambda i, j: (0, i, j)),
        out_shape=jax.ShapeDtypeStruct((r, s // r, 3 * COL), BF16),
        scratch_shapes=[pltpu.VMEM((COL // LANES, tm, LANES), F32)],
        compiler_params=_params(2),
        name=f"proj_lattice_r{r}",
    )(h, w16)


def _merge_kernel(h_ref, ca_ref, ob_ref, oc_ref, wg0_ref, wg1_ref, wg2_ref,
                  wa_ref, wb_ref, wc_ref, o_ref):
    h = h_ref[...]

    def gate(wg_ref):
        return jax.nn.sigmoid(jnp.dot(h, wg_ref[...], preferred_element_type=F32))

    def branch(y_ref, w_ref):
        return jnp.dot(y_ref[...], w_ref[...], preferred_element_type=F32)

    mixed = (gate(wg0_ref) * branch(ca_ref, wa_ref)
             + gate(wg1_ref) * branch(ob_ref, wb_ref)
             + gate(wg2_ref) * branch(oc_ref, wc_ref))
    o_ref[...] = mixed.astype(o_ref.dtype)


def gated_merge(h, ca, ob, oc, w_in16, wa16, wb16, wc16, tm=1024, tn=COL):
    s, d = h.shape
    nj = d // tn

    def gate_spec(b):
        return pl.BlockSpec((d, tn), lambda i, j: (0, GATE_BLOCK0 + b * nj + j))

    def lhs_spec(width):
        return pl.BlockSpec((tm, width), lambda i, j: (i, 0))

    def w_spec(width):
        return pl.BlockSpec((width, tn), lambda i, j: (0, j))

    return pl.pallas_call(
        _merge_kernel,
        grid=(s // tm, nj),
        in_specs=[lhs_spec(d), lhs_spec(ca.shape[1]), lhs_spec(ob.shape[1]), lhs_spec(oc.shape[1]),
                  gate_spec(0), gate_spec(1), gate_spec(2),
                  w_spec(ca.shape[1]), w_spec(ob.shape[1]), w_spec(oc.shape[1])],
        out_specs=pl.BlockSpec((tm, tn), lambda i, j: (i, j)),
        out_shape=jax.ShapeDtypeStruct((s, d), BF16),
        compiler_params=_params(2),
        name="gated_merge",
    )(h, ca, ob, oc, w_in16, w_in16, w_in16, wa16, wb16, wc16)


def _mm_residual_kernel(a_ref, w_ref, x_ref, o_ref):
    o_ref[...] = x_ref[...] + jnp.dot(a_ref[...], w_ref[...], preferred_element_type=F32)


def matmul_residual(a, w16, x, tm=1024, tn=COL):
    s, k = a.shape
    d = w16.shape[1]
    return pl.pallas_call(
        _mm_residual_kernel,
        grid=(s // tm, d // tn),
        in_specs=[pl.BlockSpec((tm, k), lambda i, j: (i, 0)),
                  pl.BlockSpec((k, tn), lambda i, j: (0, j)),
                  pl.BlockSpec((tm, tn), lambda i, j: (i, j))],
        out_specs=pl.BlockSpec((tm, tn), lambda i, j: (i, j)),
        out_shape=jax.ShapeDtypeStruct((s, d), F32),
        compiler_params=_params(2),
        name="matmul_residual",
    )(a, w16, x)


def _swiglu_kernel(h_ref, wg_ref, wu_ref, o_ref):
    h = h_ref[...]
    g = jnp.dot(h, wg_ref[...], preferred_element_type=F32)
    u = jnp.dot(h, wu_ref[...], preferred_element_type=F32)
    o_ref[...] = (jax.nn.silu(g) * u).astype(o_ref.dtype)


def swiglu_in(h, w16, tm=1024, tn=COL):
    s, d = h.shape
    nj = D_FF // tn
    return pl.pallas_call(
        _swiglu_kernel,
        grid=(s // tm, nj),
        in_specs=[pl.BlockSpec((tm, d), lambda i, j: (i, 0)),
                  pl.BlockSpec((d, tn), lambda i, j: (0, j)),
                  pl.BlockSpec((d, tn), lambda i, j: (0, nj + j))],
        out_specs=pl.BlockSpec((tm, tn), lambda i, j: (i, j)),
        out_shape=jax.ShapeDtypeStruct((s, D_FF), BF16),
        compiler_params=_params(2),
        name="swiglu_in",
    )(h, w16, w16)


def _conv_kernel(up_ref, uc_ref, un_ref, w_ref, b_ref, g_ref, beta_ref, o_ref, z_ref, *, tm, halo, chunk):
    i = pl.program_id(0)
    n = pl.num_programs(0)

    def glu(u):
        u = u.astype(F32)
        return u[:, :CONV_CH] * jax.nn.sigmoid(u[:, CONV_CH:])

    z_ref[0:halo] = jnp.where(i > 0, glu(up_ref[...]), 0.0)
    z_ref[halo:halo + tm] = glu(uc_ref[...])
    z_ref[halo + tm:2 * halo + tm] = jnp.where(i < n - 1, glu(un_ref[...]), 0.0)

    for rc in range(tm // chunk):
        base = rc * chunk + halo - CONV_HALF
        acc = jnp.zeros((chunk, CONV_CH), F32)
        for k in range(CONV_WIDTH):
            acc = acc + z_ref[base + k:base + k + chunk, :] * w_ref[k:k + 1, :]
        z = acc + b_ref[...]
        mu = jnp.mean(z, axis=-1, keepdims=True)
        zc = z - mu
        var = jnp.mean(zc * zc, axis=-1, keepdims=True)
        y = zc * lax.rsqrt(var + EPS) * g_ref[...] + beta_ref[...]
        o_ref[rc * chunk:(rc + 1) * chunk, :] = jax.nn.silu(y).astype(o_ref.dtype)


def conv_module(plain, conv_w, conv_b, cnorm_g, cnorm_b, tm=512, halo=16, chunk=64):
    s = plain.shape[0]
    nh = tm // halo
    width = 2 * CONV_CH
    row = lambda v: v.reshape(1, CONV_CH)
    vec_spec = pl.BlockSpec((1, CONV_CH), lambda i: (0, 0))
    return pl.pallas_call(
        functools.partial(_conv_kernel, tm=tm, halo=halo, chunk=chunk),
        grid=(s // tm,),
        in_specs=[pl.BlockSpec((halo, width), lambda i: (jnp.maximum(i * nh - 1, 0), 0)),
                  pl.BlockSpec((tm, width), lambda i: (i, 0)),
                  pl.BlockSpec((halo, width), lambda i: (jnp.minimum((i + 1) * nh, s // halo - 1), 0)),
                  pl.BlockSpec((CONV_WIDTH, CONV_CH), lambda i: (0, 0)),
                  vec_spec, vec_spec, vec_spec],
        out_specs=pl.BlockSpec((tm, CONV_CH), lambda i: (i, 0)),
        out_shape=jax.ShapeDtypeStruct((s, CONV_CH), BF16),
        scratch_shapes=[pltpu.VMEM((tm + 2 * halo, CONV_CH), F32)],
        compiler_params=_params(1),
        name="conv_module",
    )(plain, plain, plain, conv_w, row(conv_b), row(cnorm_g), row(cnorm_b))


def _band_tables(bq, half, step):
    shape = (bq, bq + 2 * half)
    qi = lax.broadcasted_iota(jnp.int32, shape, 0)
    kj = lax.broadcasted_iota(jnp.int32, shape, 1)
    rel = jnp.abs(kj - half - qi)
    return rel <= half, (step * rel).astype(F32), kj


def _edge_valid(band, kj, bq, half, blk, n_blocks, first, last):
    valid = band
    if first:
        valid = valid & (kj >= jnp.where(blk > 0, 0, half))
    if last:
        valid = valid & (kj < jnp.where(blk < n_blocks - 1, bq + 2 * half, bq + half))
    return valid


def _scores(q, k):
    return lax.dot_general(q, k, (((1,), (1,)), ((), ())), preferred_element_type=F32)


def _softmax_rows(sc, bias, valid, sink):
    sc = jnp.where(valid, sc * (HEAD_DIM ** -0.5) + bias, NEG)
    m = jnp.max(sc, axis=-1, keepdims=True)
    if sink is not None:
        m = jnp.maximum(m, sink)
    p = jnp.exp(sc - m)
    denom = jnp.sum(p, axis=-1, keepdims=True)
    if sink is not None:
        denom = denom + jnp.exp(sink - m)
    return p, m, denom


def _swa_kernel(slopes_ref, sink_ref, q_ref, kp_ref, kc_ref, kn_ref, vp_ref, vc_ref, vn_ref, o_ref,
                *, tq, half, n_blocks):
    i = pl.program_id(0)
    hk = pl.program_id(1)
    bq = half
    nsub = tq // bq
    kcat = jnp.concatenate([kp_ref[...], kc_ref[...], kn_ref[...]], axis=0)
    vcat = jnp.concatenate([vp_ref[...], vc_ref[...], vn_ref[...]], axis=0)
    band, dist, kj = _band_tables(bq, half, 1)
    heads = [hk * B_GROUP + g for g in range(B_GROUP)]
    biases = [-slopes_ref[hd] * dist for hd in heads]
    sinks = [sink_ref[hd] for hd in heads]
    for s in range(nsub):
        rows = slice(s * bq, (s + 1) * bq)
        win = slice(s * bq, (s + 1) * bq + 2 * half)
        valid = _edge_valid(band, kj, bq, half, i * nsub + s, n_blocks, s == 0, s == nsub - 1)
        q_all = jnp.concatenate(
            [q_ref[rows, g * HEAD_DIM:(g + 1) * HEAD_DIM] for g in range(B_GROUP)], axis=0)
        sc_all = _scores(q_all, kcat[win])
        ps, denoms = [], []
        for g in range(B_GROUP):
            p, _, denom = _softmax_rows(sc_all[g * bq:(g + 1) * bq], biases[g], valid, sinks[g])
            ps.append(p.astype(BF16))
            denoms.append(denom)
        o_all = jnp.dot(jnp.concatenate(ps, axis=0), vcat[win], preferred_element_type=F32)
        for g in range(B_GROUP):
            o = o_all[g * bq:(g + 1) * bq] / denoms[g]
            o_ref[rows, g * HEAD_DIM:(g + 1) * HEAD_DIM] = o.astype(o_ref.dtype)


def windowed_gqa(plain, slopes, sink, tq=1024):
    s = plain.shape[0]
    half = B_HALF_WINDOW
    nh = tq // half
    qw = B_GROUP * HEAD_DIM
    q_blk0 = 2 * CONV_CH // qw
    k_blk0 = (2 * CONV_CH + B_Q_HEADS * HEAD_DIM) // HEAD_DIM
    v_blk0 = k_blk0 + B_KV_HEADS

    def halo_specs(blk0):
        return [pl.BlockSpec((half, HEAD_DIM), lambda i, h: (jnp.maximum(i * nh - 1, 0), blk0 + h)),
                pl.BlockSpec((tq, HEAD_DIM), lambda i, h: (i, blk0 + h)),
                pl.BlockSpec((half, HEAD_DIM),
                             lambda i, h: (jnp.minimum((i + 1) * nh, s // half - 1), blk0 + h))]

    smem = pl.BlockSpec(memory_space=pltpu.SMEM)
    return pl.pallas_call(
        functools.partial(_swa_kernel, tq=tq, half=half, n_blocks=s // half),
        grid=(s // tq, B_KV_HEADS),
        in_specs=[smem, smem, pl.BlockSpec((tq, qw), lambda i, h: (i, q_blk0 + h))]
        + halo_specs(k_blk0) + halo_specs(v_blk0),
        out_specs=pl.BlockSpec((tq, qw), lambda i, h: (i, h)),
        out_shape=jax.ShapeDtypeStruct((s, B_Q_HEADS * HEAD_DIM), BF16),
        compiler_params=_params(2),
        name="windowed_gqa",
    )(slopes, sink, plain, plain, plain, plain, plain, plain, plain)


def _dilated_kernel(slopes_ref, *refs, tokens, bq, half, seq_len):
    o_ref, o_scr, l_scr = refs[-3:]
    i = pl.program_id(0)
    h = pl.program_id(1)
    for gi, (_, r) in enumerate(C_PATTERNS):
        q_ref, kp_ref, kc_ref, kn_ref, vp_ref, vc_ref, vn_ref = refs[7 * gi:7 * gi + 7]
        rows = tokens // r
        nsub = rows // bq
        n_blocks = seq_len // r // bq
        band, dist, kj = _band_tables(bq, half, r)
        bias = -slopes_ref[B_Q_HEADS + gi * C_HEADS_PER_GROUP + h] * dist
        for c in range(r):
            kcat = jnp.concatenate([kp_ref[c], kc_ref[c], kn_ref[c]], axis=0)
            vcat = jnp.concatenate([vp_ref[c], vc_ref[c], vn_ref[c]], axis=0)
            for s in range(nsub):
                win = slice(s * bq, (s + 1) * bq + 2 * half)
                valid = _edge_valid(band, kj, bq, half, i * nsub + s, n_blocks, s == 0, s == nsub - 1)
                p, m, denom = _softmax_rows(_scores(q_ref[c, s * bq:(s + 1) * bq, :], kcat[win]),
                                            bias, valid, None)
                o = jnp.dot(p.astype(BF16), vcat[win], preferred_element_type=F32) / denom
                lse = m + jnp.log(denom)
                dst = pl.ds(s * bq * r + c, bq, stride=r) if r > 1 else pl.ds(s * bq, bq)
                o_scr[gi, dst, :] = o
                l_scr[gi, dst, :] = jnp.broadcast_to(lse, (bq, HEAD_DIM))
    l0, l1, l2 = l_scr[0], l_scr[1], l_scr[2]
    m = jnp.maximum(jnp.maximum(l0, l1), l2)
    e0, e1, e2 = jnp.exp(l0 - m), jnp.exp(l1 - m), jnp.exp(l2 - m)
    out = (e0 * o_scr[0] + e1 * o_scr[1] + e2 * o_scr[2]) / (e0 + e1 + e2)
    o_ref[...] = out.astype(o_ref.dtype)


def dilated_attention(lattices, slopes, seq_len, tokens=C_TOKENS, bq=C_BLOCK_Q, half=C_HALF):
    nq = C_HEADS_PER_GROUP
    in_specs = [pl.BlockSpec(memory_space=pltpu.SMEM)]
    args = [slopes]
    for (_, r), lat in zip(C_PATTERNS, lattices):
        rows = tokens // r
        nh = rows // half
        last = seq_len // r // half - 1

        def specs(blk0, r=r, rows=rows, nh=nh, last=last):
            return [pl.BlockSpec((r, half, HEAD_DIM), lambda i, h: (0, jnp.maximum(i * nh - 1, 0), blk0 + h)),
                    pl.BlockSpec((r, rows, HEAD_DIM), lambda i, h: (0, i, blk0 + h)),
                    pl.BlockSpec((r, half, HEAD_DIM), lambda i, h: (0, jnp.minimum((i + 1) * nh, last), blk0 + h))]

        in_specs += [pl.BlockSpec((r, rows, HEAD_DIM), lambda i, h: (0, i, h))] + specs(nq) + specs(2 * nq)
        args += [lat] * 7
    return pl.pallas_call(
        functools.partial(_dilated_kernel, tokens=tokens, bq=bq, half=half, seq_len=seq_len),
        grid=(seq_len // tokens, nq),
        in_specs=in_specs,
        out_specs=pl.BlockSpec((tokens, HEAD_DIM), lambda i, h: (i, h)),
        out_shape=jax.ShapeDtypeStruct((seq_len, nq * HEAD_DIM), BF16),
        scratch_shapes=[pltpu.VMEM((len(C_PATTERNS), tokens, HEAD_DIM), F32),
                        pltpu.VMEM((len(C_PATTERNS), tokens, HEAD_DIM), F32)],
        compiler_params=_params(2),
        name="dilated_attention",
    )(*args)


def _alibi_slopes():
    h = jnp.arange(1, N_ATTN_HEADS + 1, dtype=F32)
    return jnp.exp2(-ALIBI_MAX_EXP * h / N_ATTN_HEADS)


def _layer(x, slopes, ln1_g, w_in, conv_w, conv_b, cnorm_g, cnorm_b, w_a, sink, w_b, w_c, w_o,
           ln2_g, w_ffn_in, w_ffn_out):
    s = x.shape[0]
    w_in16 = w_in.astype(BF16)
    h = rmsnorm(x, ln1_g, BF16)
    plain = proj_plain(h, w_in16)
    lattices = [proj_lattice(h, w_in16, gi, r) for gi, (_, r) in enumerate(C_PATTERNS)]
    ca = conv_module(plain, conv_w, conv_b, cnorm_g, cnorm_b)
    ob = windowed_gqa(plain, slopes, sink)
    oc = dilated_attention(lattices, slopes, s)
    mixed = gated_merge(h, ca, ob, oc, w_in16, w_a.astype(BF16), w_b.astype(BF16), w_c.astype(BF16))
    x = matmul_residual(mixed, w_o.astype(BF16), x)
    h2 = rmsnorm(x, ln2_g, BF16)
    act = swiglu_in(h2, w_ffn_in.astype(BF16))
    return matmul_residual(act, w_ffn_out.astype(BF16), x)


def kernel(x, ln1_g, w_in, conv_w, conv_b, cnorm_g, cnorm_b, w_a, sink, w_b, w_c, w_o, ln2_g,
           w_ffn_in, w_ffn_out, lnf_g):
    b, s, d = x.shape
    slopes = _alibi_slopes()
    outs = []
    for xb in jnp.split(x, b, axis=0):
        xb = xb.reshape(s, d)
        for l in range(ln1_g.shape[0]):
            xb = _layer(xb, slopes, ln1_g[l], w_in[l], conv_w[l], conv_b[l], cnorm_g[l], cnorm_b[l],
                        w_a[l], sink[l], w_b[l], w_c[l], w_o[l], ln2_g[l], w_ffn_in[l], w_ffn_out[l])
        outs.append(rmsnorm(xb, lnf_g, x.dtype).reshape(1, s, d))
    return outs[0] if b == 1 else jnp.concatenate(outs, axis=0)
```

```python
import functools

import jax
import jax.numpy as jnp
from jax import lax
from jax.experimental import pallas as pl
from jax.experimental.pallas import tpu as pltpu

F32 = jnp.float32
BF16 = jnp.bfloat16

D_MODEL = 2048
HEAD_DIM = 128
CONV_CH = 512
CONV_WIDTH = 31
CONV_HALF = CONV_WIDTH // 2
B_Q_HEADS = 8
B_KV_HEADS = 2
B_GROUP = B_Q_HEADS // B_KV_HEADS
B_HALF_WINDOW = 128
C_PATTERNS = ((128, 1), (512, 4), (2048, 16))
C_HEADS_PER_GROUP = 4
C_HALF = 64
N_BRANCHES = 3
N_ATTN_HEADS = B_Q_HEADS + len(C_PATTERNS) * C_HEADS_PER_GROUP
ALIBI_MAX_EXP = 8.0
D_FF = -(-8 * D_MODEL // (3 * 256)) * 256
EPS = 1e-6
NEG = -1e30

LANES = 128
SUBLANES = 8
COL = 512
PLAIN_COLS = 5 * COL
QC_BLOCK0 = 5
GATE_BLOCK0 = 14
C_TOKENS = 2048
C_BLOCK_Q = 128

VMEM_LIMIT = 56 * 1024 * 1024


def _params(n_axes):
    return pltpu.CompilerParams(
        dimension_semantics=("arbitrary",) * n_axes, vmem_limit_bytes=VMEM_LIMIT)


def _cast_rows(src_ref, dst_ref, chunk=256):
    rows = src_ref.shape[0]
    chunk = min(chunk, rows)

    def body(c, carry):
        sl = pl.ds(pl.multiple_of(c * chunk, chunk), chunk)
        dst_ref[sl, :] = src_ref[sl, :].astype(dst_ref.dtype)
        return carry

    lax.fori_loop(0, rows // chunk, body, 0)


def _rms(x, g):
    ms = jnp.mean(x * x, axis=-1, keepdims=True)
    return x * lax.rsqrt(ms + EPS) * g


def _rmsnorm_kernel(x_ref, g_ref, o_ref):
    o_ref[...] = _rms(x_ref[...], g_ref[...]).astype(o_ref.dtype)


def rmsnorm(x, g, out_dtype, tm=512):
    s, d = x.shape
    return pl.pallas_call(
        _rmsnorm_kernel,
        grid=(s // tm,),
        in_specs=[pl.BlockSpec((tm, d), lambda i: (i, 0)),
                  pl.BlockSpec((1, d), lambda i: (0, 0))],
        out_specs=pl.BlockSpec((tm, d), lambda i: (i, 0)),
        out_shape=jax.ShapeDtypeStruct((s, d), out_dtype),
        compiler_params=_params(1),
        name="rmsnorm",
    )(x, g.reshape(1, d))


def _ws_mm_kernel(a_ref, w_ref, o_ref, w16_ref):
    @pl.when(pl.program_id(1) == 0)
    def _():
        _cast_rows(w_ref, w16_ref)

    o_ref[...] = jnp.dot(a_ref[...], w16_ref[...],
                         preferred_element_type=F32).astype(o_ref.dtype)


def proj_plain(h, w_in, layer, tm=1024, tn=PLAIN_COLS // 2):
    s, d = h.shape
    return pl.pallas_call(
        _ws_mm_kernel,
        grid=(PLAIN_COLS // tn, s // tm),
        in_specs=[pl.BlockSpec((tm, d), lambda n, m: (m, 0)),
                  pl.BlockSpec((None, d, tn), lambda n, m: (layer, 0, n))],
        out_specs=pl.BlockSpec((tm, tn), lambda n, m: (m, n)),
        out_shape=jax.ShapeDtypeStruct((s, PLAIN_COLS), BF16),
        scratch_shapes=[pltpu.VMEM((d, tn), BF16)],
        compiler_params=_params(2),
        name="proj_plain",
    )(h, w_in)


def _lattice_kernel(a_ref, wq_ref, wk_ref, wv_ref, o_ref, w16_ref, acc_ref, *, r):
    @pl.when(pl.program_id(0) == 0)
    def _():
        for j, w_ref in enumerate((wq_ref, wk_ref, wv_ref)):
            _cast_rows(w_ref, w16_ref.at[j])

    a = a_ref[...]
    rows = a.shape[0] // r
    slabs = COL // LANES
    for j in range(3):
        acc = jnp.dot(a, w16_ref[j], preferred_element_type=F32)
        if r == 1:
            o_ref[0, :, j * COL:(j + 1) * COL] = acc.astype(o_ref.dtype)
            continue
        for s in range(slabs):
            acc_ref[j * slabs + s] = acc[:, s * LANES:(s + 1) * LANES]
        for c in range(r):
            for s in range(slabs):
                col = j * COL + s * LANES
                o_ref[c, :, col:col + LANES] = (
                    acc_ref[j * slabs + s, pl.ds(c, rows, stride=r), :].astype(o_ref.dtype))


def proj_lattice(h, w_in, layer, group, r, tm=1024):
    s, d = h.shape

    def w_spec(j):
        return pl.BlockSpec((None, d, COL), lambda m: (layer, 0, QC_BLOCK0 + group + 3 * j),
                            pipeline_mode=pl.Buffered(1))

    return pl.pallas_call(
        functools.partial(_lattice_kernel, r=r),
        grid=(s // tm,),
        in_specs=[pl.BlockSpec((tm, d), lambda m: (m, 0)), w_spec(0), w_spec(1), w_spec(2)],
        out_specs=pl.BlockSpec((r, tm // r, 3 * COL), lambda m: (0, m, 0)),
        out_shape=jax.ShapeDtypeStruct((r, s // r, 3 * COL), BF16),
        scratch_shapes=[pltpu.VMEM((3, d, COL), BF16),
                        pltpu.VMEM((3 * COL // LANES, tm, LANES), F32)],
        compiler_params=_params(1),
        name=f"proj_lattice_r{r}",
    )(h, w_in, w_in, w_in)


def _merge_kernel(h_ref, ca_ref, ob_ref, oc_ref, wg0_ref, wg1_ref, wg2_ref, wa_ref, wb_ref, wc_ref,
                  o_ref, wg16_ref, wa16_ref, wb16_ref, wc16_ref):
    @pl.when(pl.program_id(1) == 0)
    def _():
        for b, wg_ref in enumerate((wg0_ref, wg1_ref, wg2_ref)):
            _cast_rows(wg_ref, wg16_ref.at[b])
        _cast_rows(wa_ref, wa16_ref)
        _cast_rows(wb_ref, wb16_ref)
        _cast_rows(wc_ref, wc16_ref)

    h = h_ref[...]

    def gate(b):
        return jax.nn.sigmoid(jnp.dot(h, wg16_ref[b], preferred_element_type=F32))

    def branch(y_ref, w16_ref):
        return jnp.dot(y_ref[...], w16_ref[...], preferred_element_type=F32)

    mixed = (gate(0) * branch(ca_ref, wa16_ref)
             + gate(1) * branch(ob_ref, wb16_ref)
             + gate(2) * branch(oc_ref, wc16_ref))
    o_ref[...] = mixed.astype(o_ref.dtype)


def gated_merge(h, ca, ob, oc, w_in, w_a, w_b, w_c, layer, tm=512, tn=COL):
    s, d = h.shape
    nj = d // tn
    widths = (ca.shape[1], ob.shape[1], oc.shape[1])

    def gate_spec(b):
        return pl.BlockSpec((None, d, tn), lambda n, m: (layer, 0, GATE_BLOCK0 + b * nj + n))

    def lhs_spec(width):
        return pl.BlockSpec((tm, width), lambda n, m: (m, 0))

    def w_spec(width):
        return pl.BlockSpec((None, width, tn), lambda n, m: (layer, 0, n))

    return pl.pallas_call(
        _merge_kernel,
        grid=(nj, s // tm),
        in_specs=[lhs_spec(d)] + [lhs_spec(w) for w in widths]
        + [gate_spec(0), gate_spec(1), gate_spec(2)] + [w_spec(w) for w in widths],
        out_specs=pl.BlockSpec((tm, tn), lambda n, m: (m, n)),
        out_shape=jax.ShapeDtypeStruct((s, d), BF16),
        scratch_shapes=[pltpu.VMEM((N_BRANCHES, d, tn), BF16)]
        + [pltpu.VMEM((w, tn), BF16) for w in widths],
        compiler_params=_params(2),
        name="gated_merge",
    )(h, ca, ob, oc, w_in, w_in, w_in, w_a, w_b, w_c)


def _out_proj_kernel(a_ref, w_ref, x_ref, g_ref, xo_ref, ho_ref, w16_ref):
    @pl.when(pl.program_id(0) == 0)
    def _():
        _cast_rows(w_ref, w16_ref)

    x = x_ref[...] + jnp.dot(a_ref[...], w16_ref[...], preferred_element_type=F32)
    xo_ref[...] = x
    ho_ref[...] = _rms(x, g_ref[...]).astype(ho_ref.dtype)


def out_proj_norm(a, w_o, layer, x, g, tm=512):
    s, k = a.shape
    d = x.shape[1]
    row = pl.BlockSpec((tm, d), lambda m: (m, 0))
    return pl.pallas_call(
        _out_proj_kernel,
        grid=(s // tm,),
        in_specs=[pl.BlockSpec((tm, k), lambda m: (m, 0)),
                  pl.BlockSpec((None, k, d), lambda m: (layer, 0, 0), pipeline_mode=pl.Buffered(1)),
                  row,
                  pl.BlockSpec((1, d), lambda m: (0, 0))],
        out_specs=[row, row],
        out_shape=[jax.ShapeDtypeStruct((s, d), F32), jax.ShapeDtypeStruct((s, d), BF16)],
        scratch_shapes=[pltpu.VMEM((k, d), BF16)],
        compiler_params=_params(1),
        name="out_proj_norm",
    )(a, w_o, x, g.reshape(1, d))


def _swiglu_kernel(h_ref, wg_ref, wu_ref, o_ref, w16_ref):
    @pl.when(pl.program_id(1) == 0)
    def _():
        _cast_rows(wg_ref, w16_ref.at[0])
        _cast_rows(wu_ref, w16_ref.at[1])

    h = h_ref[...]
    g = jnp.dot(h, w16_ref[0], preferred_element_type=F32)
    u = jnp.dot(h, w16_ref[1], preferred_element_type=F32)
    o_ref[...] = (jax.nn.silu(g) * u).astype(o_ref.dtype)


def swiglu_in(h, w_ffn_in, layer, tm=1024, tn=COL):
    s, d = h.shape
    nj = D_FF // tn
    return pl.pallas_call(
        _swiglu_kernel,
        grid=(nj, s // tm),
        in_specs=[pl.BlockSpec((tm, d), lambda n, m: (m, 0)),
                  pl.BlockSpec((None, d, tn), lambda n, m: (layer, 0, n)),
                  pl.BlockSpec((None, d, tn), lambda n, m: (layer, 0, nj + n))],
        out_specs=pl.BlockSpec((tm, tn), lambda n, m: (m, n)),
        out_shape=jax.ShapeDtypeStruct((s, D_FF), BF16),
        scratch_shapes=[pltpu.VMEM((2, d, tn), BF16)],
        compiler_params=_params(2),
        name="swiglu_in",
    )(h, w_ffn_in, w_ffn_in)


def _mm_residual_kernel(a_ref, w_ref, x_ref, o_ref):
    o_ref[...] = x_ref[...] + jnp.dot(a_ref[...], w_ref[...], preferred_element_type=F32)


def matmul_residual(a, w16, x, tm=1024, tn=COL):
    s, k = a.shape
    d = w16.shape[1]
    return pl.pallas_call(
        _mm_residual_kernel,
        grid=(s // tm, d // tn),
        in_specs=[pl.BlockSpec((tm, k), lambda i, j: (i, 0)),
                  pl.BlockSpec((k, tn), lambda i, j: (0, j)),
                  pl.BlockSpec((tm, tn), lambda i, j: (i, j))],
        out_specs=pl.BlockSpec((tm, tn), lambda i, j: (i, j)),
        out_shape=jax.ShapeDtypeStruct((s, d), F32),
        compiler_params=_params(2),
        name="matmul_residual",
    )(a, w16, x)


def _conv_kernel(up_ref, uc_ref, un_ref, w_ref, b_ref, g_ref, beta_ref, o_ref, z_ref, sh_ref,
                 *, tm, halo, chunk):
    i = pl.program_id(0)
    n = pl.num_programs(0)

    def glu(u):
        u = u.astype(F32)
        return u[:, :CONV_CH] * jax.nn.sigmoid(u[:, CONV_CH:])

    z_ref[0:halo] = jnp.where(i > 0, glu(up_ref[...]), 0.0)
    z_ref[halo:halo + tm] = glu(uc_ref[...])
    z_ref[halo + tm:2 * halo + tm] = jnp.where(i < n - 1, glu(un_ref[...]), 0.0)
    span = sh_ref.shape[1]
    for j in range(1, SUBLANES):
        sh_ref[j - 1] = z_ref[j:j + span, :]

    for rc in range(tm // chunk):
        acc = jnp.zeros((chunk, CONV_CH), F32)
        for k in range(CONV_WIDTH):
            off = rc * chunk + halo - CONV_HALF + k
            j = off % SUBLANES
            src = z_ref if j == 0 else sh_ref.at[j - 1]
            acc = acc + src[off - j:off - j + chunk, :] * w_ref[k:k + 1, :]
        z = acc + b_ref[...]
        mu = jnp.mean(z, axis=-1, keepdims=True)
        zc = z - mu
        var = jnp.mean(zc * zc, axis=-1, keepdims=True)
        y = zc * lax.rsqrt(var + EPS) * g_ref[...] + beta_ref[...]
        o_ref[rc * chunk:(rc + 1) * chunk, :] = jax.nn.silu(y).astype(o_ref.dtype)


def conv_module(plain, conv_w, conv_b, cnorm_g, cnorm_b, layer, tm=512, halo=16, chunk=64):
    s = plain.shape[0]
    nh = tm // halo
    width = 2 * CONV_CH
    vec_spec = pl.BlockSpec((None, 1, CONV_CH), lambda i: (layer, 0, 0))
    row = lambda v: v.reshape(v.shape[0], 1, CONV_CH)
    return pl.pallas_call(
        functools.partial(_conv_kernel, tm=tm, halo=halo, chunk=chunk),
        grid=(s // tm,),
        in_specs=[pl.BlockSpec((halo, width), lambda i: (jnp.maximum(i * nh - 1, 0), 0)),
                  pl.BlockSpec((tm, width), lambda i: (i, 0)),
                  pl.BlockSpec((halo, width), lambda i: (jnp.minimum((i + 1) * nh, s // halo - 1), 0)),
                  pl.BlockSpec((None, CONV_WIDTH, CONV_CH), lambda i: (layer, 0, 0)),
                  vec_spec, vec_spec, vec_spec],
        out_specs=pl.BlockSpec((tm, CONV_CH), lambda i: (i, 0)),
        out_shape=jax.ShapeDtypeStruct((s, CONV_CH), BF16),
        scratch_shapes=[pltpu.VMEM((tm + 2 * halo, CONV_CH), F32),
                        pltpu.VMEM((SUBLANES - 1, tm + 2 * halo - SUBLANES, CONV_CH), F32)],
        compiler_params=_params(1),
        name="conv_module",
    )(plain, plain, plain, conv_w, row(conv_b), row(cnorm_g), row(cnorm_b))


def _band_tables(bq, half, step):
    shape = (bq, bq + 2 * half)
    qi = lax.broadcasted_iota(jnp.int32, shape, 0)
    kj = lax.broadcasted_iota(jnp.int32, shape, 1)
    rel = jnp.abs(kj - half - qi)
    return rel <= half, (step * rel).astype(F32), kj


def _edge_valid(band, kj, bq, half, blk, n_blocks, first, last):
    valid = band
    if first:
        valid = valid & (kj >= jnp.where(blk > 0, 0, half))
    if last:
        valid = valid & (kj < jnp.where(blk < n_blocks - 1, bq + 2 * half, bq + half))
    return valid


def _scores(q, k):
    return lax.dot_general(q, k, (((1,), (1,)), ((), ())), preferred_element_type=F32)


def _softmax_rows(sc, bias, valid, sink):
    sc = jnp.where(valid, sc * (HEAD_DIM ** -0.5) + bias, NEG)
    m = jnp.max(sc, axis=-1, keepdims=True)
    if sink is not None:
        m = jnp.maximum(m, sink)
    p = jnp.exp(sc - m)
    denom = jnp.sum(p, axis=-1, keepdims=True)
    if sink is not None:
        denom = denom + jnp.exp(sink - m)
    return p, m, denom


def _swa_kernel(slopes_ref, sink_ref, q_ref, kp_ref, kc_ref, kn_ref, vp_ref, vc_ref, vn_ref, o_ref,
                *, tq, half, n_blocks, layer):
    i = pl.program_id(0)
    hk = pl.program_id(1)
    bq = half
    nsub = tq // bq
    kcat = jnp.concatenate([kp_ref[...], kc_ref[...], kn_ref[...]], axis=0)
    vcat = jnp.concatenate([vp_ref[...], vc_ref[...], vn_ref[...]], axis=0)
    band, dist, kj = _band_tables(bq, half, 1)
    heads = [hk * B_GROUP + g for g in range(B_GROUP)]
    biases = [-slopes_ref[hd] * dist for hd in heads]
    sinks = [sink_ref[layer, hd] for hd in heads]
    for s in range(nsub):
        rows = slice(s * bq, (s + 1) * bq)
        win = slice(s * bq, (s + 1) * bq + 2 * half)
        valid = _edge_valid(band, kj, bq, half, i * nsub + s, n_blocks, s == 0, s == nsub - 1)
        q_all = jnp.concatenate(
            [q_ref[rows, g * HEAD_DIM:(g + 1) * HEAD_DIM] for g in range(B_GROUP)], axis=0)
        sc_all = _scores(q_all, kcat[win])
        ps, denoms = [], []
        for g in range(B_GROUP):
            p, _, denom = _softmax_rows(sc_all[g * bq:(g + 1) * bq], biases[g], valid, sinks[g])
            ps.append(p.astype(BF16))
            denoms.append(denom)
        o_all = jnp.dot(jnp.concatenate(ps, axis=0), vcat[win], preferred_element_type=F32)
        for g in range(B_GROUP):
            o = o_all[g * bq:(g + 1) * bq] / denoms[g]
            o_ref[rows, g * HEAD_DIM:(g + 1) * HEAD_DIM] = o.astype(o_ref.dtype)


def windowed_gqa(plain, slopes, sink, layer, tq=1024):
    s = plain.shape[0]
    half = B_HALF_WINDOW
    nh = tq // half
    qw = B_GROUP * HEAD_DIM
    q_blk0 = 2 * CONV_CH // qw
    k_blk0 = (2 * CONV_CH + B_Q_HEADS * HEAD_DIM) // HEAD_DIM
    v_blk0 = k_blk0 + B_KV_HEADS

    def halo_specs(blk0):
        return [pl.BlockSpec((half, HEAD_DIM), lambda i, h: (jnp.maximum(i * nh - 1, 0), blk0 + h)),
                pl.BlockSpec((tq, HEAD_DIM), lambda i, h: (i, blk0 + h)),
                pl.BlockSpec((half, HEAD_DIM),
                             lambda i, h: (jnp.minimum((i + 1) * nh, s // half - 1), blk0 + h))]

    smem = pl.BlockSpec(memory_space=pltpu.SMEM)
    return pl.pallas_call(
        functools.partial(_swa_kernel, tq=tq, half=half, n_blocks=s // half, layer=layer),
        grid=(s // tq, B_KV_HEADS),
        in_specs=[smem, smem, pl.BlockSpec((tq, qw), lambda i, h: (i, q_blk0 + h))]
        + halo_specs(k_blk0) + halo_specs(v_blk0),
        out_specs=pl.BlockSpec((tq, qw), lambda i, h: (i, h)),
        out_shape=jax.ShapeDtypeStruct((s, B_Q_HEADS * HEAD_DIM), BF16),
        compiler_params=_params(2),
        name="windowed_gqa",
    )(slopes, sink, plain, plain, plain, plain, plain, plain, plain)


def _dilated_kernel(slopes_ref, *refs, tokens, bq, half, seq_len):
    o_ref, o_scr, l_scr = refs[-3:]
    i = pl.program_id(0)
    h = pl.program_id(1)
    for gi, (_, r) in enumerate(C_PATTERNS):
        q_ref, kp_ref, kc_ref, kn_ref, vp_ref, vc_ref, vn_ref = refs[7 * gi:7 * gi + 7]
        rows = tokens // r
        nsub = rows // bq
        n_blocks = seq_len // r // bq
        band, dist, kj = _band_tables(bq, half, r)
        bias = -slopes_ref[B_Q_HEADS + gi * C_HEADS_PER_GROUP + h] * dist
        for c in range(r):
            kcat = jnp.concatenate([kp_ref[c], kc_ref[c], kn_ref[c]], axis=0)
            vcat = jnp.concatenate([vp_ref[c], vc_ref[c], vn_ref[c]], axis=0)
            for s in range(nsub):
                win = slice(s * bq, (s + 1) * bq + 2 * half)
                valid = _edge_valid(band, kj, bq, half, i * nsub + s, n_blocks, s == 0, s == nsub - 1)
                p, m, denom = _softmax_rows(_scores(q_ref[c, s * bq:(s + 1) * bq, :], kcat[win]),
                                            bias, valid, None)
                o = jnp.dot(p.astype(BF16), vcat[win], preferred_element_type=F32) / denom
                lse = m + jnp.log(denom)
                dst = pl.ds(s * bq * r + c, bq, stride=r) if r > 1 else pl.ds(s * bq, bq)
                o_scr[gi, dst, :] = o
                l_scr[gi, dst, :] = jnp.broadcast_to(lse, (bq, HEAD_DIM))
    l0, l1, l2 = l_scr[0], l_scr[1], l_scr[2]
    m = jnp.maximum(jnp.maximum(l0, l1), l2)
    e0, e1, e2 = jnp.exp(l0 - m), jnp.exp(l1 - m), jnp.exp(l2 - m)
    out = (e0 * o_scr[0] + e1 * o_scr[1] + e2 * o_scr[2]) / (e0 + e1 + e2)
    o_ref[...] = out.astype(o_ref.dtype)


def dilated_attention(lattices, slopes, seq_len, tokens=C_TOKENS, bq=C_BLOCK_Q, half=C_HALF):
    nq = C_HEADS_PER_GROUP
    in_specs = [pl.BlockSpec(memory_space=pltpu.SMEM)]
    args = [slopes]
    for (_, r), lat in zip(C_PATTERNS, lattices):
        rows = tokens // r
        nh = rows // half
        last = seq_len // r // half - 1

        def specs(blk0, r=r, rows=rows, nh=nh, last=last):
            return [pl.BlockSpec((r, half, HEAD_DIM), lambda i, h: (0, jnp.maximum(i * nh - 1, 0), blk0 + h)),
                    pl.BlockSpec((r, rows, HEAD_DIM), lambda i, h: (0, i, blk0 + h)),
                    pl.BlockSpec((r, half, HEAD_DIM), lambda i, h: (0, jnp.minimum((i + 1) * nh, last), blk0 + h))]

        in_specs += [pl.BlockSpec((r, rows, HEAD_DIM), lambda i, h: (0, i, h))] + specs(nq) + specs(2 * nq)
        args += [lat] * 7
    return pl.pallas_call(
        functools.partial(_dilated_kernel, tokens=tokens, bq=bq, half=half, seq_len=seq_len),
        grid=(seq_len // tokens, nq),
        in_specs=in_specs,
        out_specs=pl.BlockSpec((tokens, HEAD_DIM), lambda i, h: (i, h)),
        out_shape=jax.ShapeDtypeStruct((seq_len, nq * HEAD_DIM), BF16),
        scratch_shapes=[pltpu.VMEM((len(C_PATTERNS), tokens, HEAD_DIM), F32),
                        pltpu.VMEM((len(C_PATTERNS), tokens, HEAD_DIM), F32)],
        compiler_params=_params(2),
        name="dilated_attention",
    )(*args)


def _alibi_slopes():
    h = jnp.arange(1, N_ATTN_HEADS + 1, dtype=F32)
    return jnp.exp2(-ALIBI_MAX_EXP * h / N_ATTN_HEADS)


def kernel(x, ln1_g, w_in, conv_w, conv_b, cnorm_g, cnorm_b, w_a, sink, w_b, w_c, w_o, ln2_g,
           w_ffn_in, w_ffn_out, lnf_g):
    b, s, d = x.shape
    depth = ln1_g.shape[0]
    slopes = _alibi_slopes()
    outs = []
    for xb in jnp.split(x, b, axis=0):
        xb = xb.reshape(s, d)
        for l in range(depth):
            h = rmsnorm(xb, ln1_g[l], BF16)
            plain = proj_plain(h, w_in, l)
            lattices = [proj_lattice(h, w_in, l, gi, r) for gi, (_, r) in enumerate(C_PATTERNS)]
            ca = conv_module(plain, conv_w, conv_b, cnorm_g, cnorm_b, l)
            ob = windowed_gqa(plain, slopes, sink, l)
            oc = dilated_attention(lattices, slopes, s)
            mixed = gated_merge(h, ca, ob, oc, w_in, w_a, w_b, w_c, l)
            xb, h2 = out_proj_norm(mixed, w_o, l, xb, ln2_g[l])
            act = swiglu_in(h2, w_ffn_in, l)
            xb = matmul_residual(act, w_ffn_out[l].astype(BF16), xb)
        outs.append(rmsnorm(xb, lnf_g, x.dtype).reshape(1, s, d))
    return outs[0] if b == 1 else jnp.concatenate(outs, axis=0)
```

```python
import functools

import jax
import jax.numpy as jnp
from jax import lax
from jax.experimental import pallas as pl
from jax.experimental.pallas import tpu as pltpu

F32 = jnp.float32
BF16 = jnp.bfloat16

D_MODEL = 2048
HEAD_DIM = 128
CONV_CH = 512
CONV_WIDTH = 31
CONV_HALF = CONV_WIDTH // 2
B_Q_HEADS = 8
B_KV_HEADS = 2
B_GROUP = B_Q_HEADS // B_KV_HEADS
B_HALF_WINDOW = 128
C_PATTERNS = ((128, 1), (512, 4), (2048, 16))
C_HEADS_PER_GROUP = 4
C_HALF = 64
N_BRANCHES = 3
N_ATTN_HEADS = B_Q_HEADS + len(C_PATTERNS) * C_HEADS_PER_GROUP
ALIBI_MAX_EXP = 8.0
D_FF = -(-8 * D_MODEL // (3 * 256)) * 256
EPS = 1e-6
NEG = -1e30
LOG2E = 1.4426950408889634
LN2 = 0.6931471805599453

LANES = 128
MXU_COLS = 256
SUBLANES = 8
LATTICE_STRIDE = 4
COL = 512
PLAIN_COLS = 5 * COL
QC_BLOCK0 = 5
GATE_BLOCK0 = 14
C_TOKENS = 2048
C_BLOCK_Q = 128

VMEM_LIMIT = 56 * 1024 * 1024


def _params(n_axes):
    return pltpu.CompilerParams(
        dimension_semantics=("arbitrary",) * n_axes, vmem_limit_bytes=VMEM_LIMIT)


def _cast_rows(src_ref, dst_ref, chunk=256):
    rows = src_ref.shape[0]
    chunk = min(chunk, rows)

    def body(c, carry):
        sl = pl.ds(pl.multiple_of(c * chunk, chunk), chunk)
        dst_ref[sl, :] = src_ref[sl, :].astype(dst_ref.dtype)
        return carry

    lax.fori_loop(0, rows // chunk, body, 0)


def _rms(x, g):
    ms = jnp.mean(x * x, axis=-1, keepdims=True)
    return x * lax.rsqrt(ms + EPS) * g


def _rmsnorm_kernel(x_ref, g_ref, o_ref):
    o_ref[...] = _rms(x_ref[...], g_ref[...]).astype(o_ref.dtype)


def rmsnorm(x, g, out_dtype, tm=512):
    s, d = x.shape
    return pl.pallas_call(
        _rmsnorm_kernel,
        grid=(s // tm,),
        in_specs=[pl.BlockSpec((tm, d), lambda i: (i, 0)),
                  pl.BlockSpec((1, d), lambda i: (0, 0))],
        out_specs=pl.BlockSpec((tm, d), lambda i: (i, 0)),
        out_shape=jax.ShapeDtypeStruct((s, d), out_dtype),
        compiler_params=_params(1),
        name="rmsnorm",
    )(x, g.reshape(1, d))


def _ws_mm_kernel(a_ref, w_ref, o_ref, w16_ref):
    @pl.when(pl.program_id(1) == 0)
    def _():
        _cast_rows(w_ref, w16_ref)

    o_ref[...] = jnp.dot(a_ref[...], w16_ref[...],
                         preferred_element_type=F32).astype(o_ref.dtype)


def proj_plain(h, w_in, layer, tm=1024, tn=PLAIN_COLS // 2):
    s, d = h.shape
    return pl.pallas_call(
        _ws_mm_kernel,
        grid=(PLAIN_COLS // tn, s // tm),
        in_specs=[pl.BlockSpec((tm, d), lambda n, m: (m, 0)),
                  pl.BlockSpec((None, d, tn), lambda n, m: (layer, 0, n))],
        out_specs=pl.BlockSpec((tm, tn), lambda n, m: (m, n)),
        out_shape=jax.ShapeDtypeStruct((s, PLAIN_COLS), BF16),
        scratch_shapes=[pltpu.VMEM((d, tn), BF16)],
        compiler_params=_params(2),
        name="proj_plain",
    )(h, w_in)


def _lattice_kernel(a_ref, wq_ref, wk_ref, wv_ref, o_ref, w16_ref, acc_ref, tmp_ref, *, r):
    @pl.when(pl.program_id(0) == 0)
    def _():
        for j, w_ref in enumerate((wq_ref, wk_ref, wv_ref)):
            _cast_rows(w_ref, w16_ref.at[j])

    a = a_ref[...]
    rows = a.shape[0] // r
    slabs = COL // LANES
    for j in range(3):
        acc = jnp.dot(a, w16_ref[j], preferred_element_type=F32)
        if r == 1:
            o_ref[0, :, j * COL:(j + 1) * COL] = acc.astype(o_ref.dtype)
            continue
        for s in range(slabs):
            acc_ref[j * slabs + s] = acc[:, s * LANES:(s + 1) * LANES]
        for s in range(slabs):
            slab = acc_ref.at[j * slabs + s]
            col = j * COL + s * LANES
            if r == LATTICE_STRIDE:
                for c in range(r):
                    o_ref[c, :, col:col + LANES] = slab[pl.ds(c, rows, stride=r), :].astype(o_ref.dtype)
                continue
            assert r == LATTICE_STRIDE ** 2
            tmp = tmp_ref.at[j * slabs + s]
            for c0 in range(LATTICE_STRIDE):
                tmp[c0] = slab[pl.ds(c0, rows * LATTICE_STRIDE, stride=LATTICE_STRIDE), :]
            for c0 in range(LATTICE_STRIDE):
                for c1 in range(LATTICE_STRIDE):
                    o_ref[c0 + LATTICE_STRIDE * c1, :, col:col + LANES] = (
                        tmp[c0, pl.ds(c1, rows, stride=LATTICE_STRIDE), :].astype(o_ref.dtype))


def proj_lattice(h, w_in, layer, group, r, tm=1024):
    s, d = h.shape
    n_slabs = 3 * COL // LANES
    two_pass_rows = tm // LATTICE_STRIDE if r == LATTICE_STRIDE ** 2 else SUBLANES

    def w_spec(j):
        return pl.BlockSpec((None, d, COL), lambda m: (layer, 0, QC_BLOCK0 + group + 3 * j),
                            pipeline_mode=pl.Buffered(1))

    return pl.pallas_call(
        functools.partial(_lattice_kernel, r=r),
        grid=(s // tm,),
        in_specs=[pl.BlockSpec((tm, d), lambda m: (m, 0)), w_spec(0), w_spec(1), w_spec(2)],
        out_specs=pl.BlockSpec((r, tm // r, 3 * COL), lambda m: (0, m, 0)),
        out_shape=jax.ShapeDtypeStruct((r, s // r, 3 * COL), BF16),
        scratch_shapes=[pltpu.VMEM((3, d, COL), BF16),
                        pltpu.VMEM((n_slabs, tm, LANES), F32),
                        pltpu.VMEM((n_slabs, LATTICE_STRIDE, two_pass_rows, LANES), F32)],
        compiler_params=_params(1),
        name=f"proj_lattice_r{r}",
    )(h, w_in, w_in, w_in)


def _merge_kernel(h_ref, ca_ref, ob_ref, oc_ref, wg0_ref, wg1_ref, wg2_ref, wa_ref, wb_ref, wc_ref,
                  o_ref, wg16_ref, wa16_ref, wb16_ref, wc16_ref):
    @pl.when(pl.program_id(1) == 0)
    def _():
        for b, wg_ref in enumerate((wg0_ref, wg1_ref, wg2_ref)):
            _cast_rows(wg_ref, wg16_ref.at[b])
        _cast_rows(wa_ref, wa16_ref)
        _cast_rows(wb_ref, wb16_ref)
        _cast_rows(wc_ref, wc16_ref)

    h = h_ref[...]
    for c0 in range(0, o_ref.shape[1], MXU_COLS):
        cols = slice(c0, c0 + MXU_COLS)

        def gate(b):
            return jax.nn.sigmoid(jnp.dot(h, wg16_ref[b, :, cols], preferred_element_type=F32))

        def branch(y_ref, w16_ref):
            return jnp.dot(y_ref[...], w16_ref[:, cols], preferred_element_type=F32)

        mixed = (gate(0) * branch(ca_ref, wa16_ref)
                 + gate(1) * branch(ob_ref, wb16_ref)
                 + gate(2) * branch(oc_ref, wc16_ref))
        o_ref[:, cols] = mixed.astype(o_ref.dtype)


def gated_merge(h, ca, ob, oc, w_in, w_a, w_b, w_c, layer, tm=512, tn=COL):
    s, d = h.shape
    nj = d // tn
    widths = (ca.shape[1], ob.shape[1], oc.shape[1])

    def gate_spec(b):
        return pl.BlockSpec((None, d, tn), lambda n, m: (layer, 0, GATE_BLOCK0 + b * nj + n))

    def lhs_spec(width):
        return pl.BlockSpec((tm, width), lambda n, m: (m, 0))

    def w_spec(width):
        return pl.BlockSpec((None, width, tn), lambda n, m: (layer, 0, n))

    return pl.pallas_call(
        _merge_kernel,
        grid=(nj, s // tm),
        in_specs=[lhs_spec(d)] + [lhs_spec(w) for w in widths]
        + [gate_spec(0), gate_spec(1), gate_spec(2)] + [w_spec(w) for w in widths],
        out_specs=pl.BlockSpec((tm, tn), lambda n, m: (m, n)),
        out_shape=jax.ShapeDtypeStruct((s, d), BF16),
        scratch_shapes=[pltpu.VMEM((N_BRANCHES, d, tn), BF16)]
        + [pltpu.VMEM((w, tn), BF16) for w in widths],
        compiler_params=_params(2),
        name="gated_merge",
    )(h, ca, ob, oc, w_in, w_in, w_in, w_a, w_b, w_c)


def _out_proj_kernel(a_ref, w_ref, x_ref, g_ref, xo_ref, ho_ref, w16_ref):
    @pl.when(pl.program_id(0) == 0)
    def _():
        _cast_rows(w_ref, w16_ref)

    x = x_ref[...] + jnp.dot(a_ref[...], w16_ref[...], preferred_element_type=F32)
    xo_ref[...] = x
    ho_ref[...] = _rms(x, g_ref[...]).astype(ho_ref.dtype)


def out_proj_norm(a, w_o, layer, x, g, tm=512):
    s, k = a.shape
    d = x.shape[1]
    row = pl.BlockSpec((tm, d), lambda m: (m, 0))
    return pl.pallas_call(
        _out_proj_kernel,
        grid=(s // tm,),
        in_specs=[pl.BlockSpec((tm, k), lambda m: (m, 0)),
                  pl.BlockSpec((None, k, d), lambda m: (layer, 0, 0), pipeline_mode=pl.Buffered(1)),
                  row,
                  pl.BlockSpec((1, d), lambda m: (0, 0))],
        out_specs=[row, row],
        out_shape=[jax.ShapeDtypeStruct((s, d), F32), jax.ShapeDtypeStruct((s, d), BF16)],
        scratch_shapes=[pltpu.VMEM((k, d), BF16)],
        compiler_params=_params(1),
        name="out_proj_norm",
    )(a, w_o, x, g.reshape(1, d))


def _swiglu_kernel(h_ref, wg_ref, wu_ref, o_ref, w16_ref):
    @pl.when(pl.program_id(1) == 0)
    def _():
        _cast_rows(wg_ref, w16_ref.at[0])
        _cast_rows(wu_ref, w16_ref.at[1])

    h = h_ref[...]
    for c0 in range(0, o_ref.shape[1], MXU_COLS):
        cols = slice(c0, c0 + MXU_COLS)
        g = jnp.dot(h, w16_ref[0, :, cols], preferred_element_type=F32)
        u = jnp.dot(h, w16_ref[1, :, cols], preferred_element_type=F32)
        o_ref[:, cols] = (jax.nn.silu(g) * u).astype(o_ref.dtype)


def swiglu_in(h, w_ffn_in, layer, tm=2048, tn=COL):
    s, d = h.shape
    nj = D_FF // tn
    return pl.pallas_call(
        _swiglu_kernel,
        grid=(nj, s // tm),
        in_specs=[pl.BlockSpec((tm, d), lambda n, m: (m, 0)),
                  pl.BlockSpec((None, d, tn), lambda n, m: (layer, 0, n)),
                  pl.BlockSpec((None, d, tn), lambda n, m: (layer, 0, nj + n))],
        out_specs=pl.BlockSpec((tm, tn), lambda n, m: (m, n)),
        out_shape=jax.ShapeDtypeStruct((s, D_FF), BF16),
        scratch_shapes=[pltpu.VMEM((2, d, tn), BF16)],
        compiler_params=_params(2),
        name="swiglu_in",
    )(h, w_ffn_in, w_ffn_in)


def _mm_residual_kernel(a_ref, w_ref, x_ref, o_ref):
    o_ref[...] = x_ref[...] + jnp.dot(a_ref[...], w_ref[...], preferred_element_type=F32)


def matmul_residual(a, w16, x, tm=1024, tn=COL):
    s, k = a.shape
    d = w16.shape[1]
    return pl.pallas_call(
        _mm_residual_kernel,
        grid=(s // tm, d // tn),
        in_specs=[pl.BlockSpec((tm, k), lambda i, j: (i, 0)),
                  pl.BlockSpec((k, tn), lambda i, j: (0, j)),
                  pl.BlockSpec((tm, tn), lambda i, j: (i, j))],
        out_specs=pl.BlockSpec((tm, tn), lambda i, j: (i, j)),
        out_shape=jax.ShapeDtypeStruct((s, d), F32),
        compiler_params=_params(2),
        name="matmul_residual",
    )(a, w16, x)


def _conv_kernel(up_ref, uc_ref, un_ref, w_ref, b_ref, g_ref, beta_ref, o_ref, z_ref, sh_ref,
                 *, tm, halo, chunk):
    i = pl.program_id(0)
    n = pl.num_programs(0)

    def glu(u):
        u = u.astype(F32)
        return u[:, :CONV_CH] * jax.nn.sigmoid(u[:, CONV_CH:])

    z_ref[0:halo] = jnp.where(i > 0, glu(up_ref[...]), 0.0)
    z_ref[halo:halo + tm] = glu(uc_ref[...])
    z_ref[halo + tm:2 * halo + tm] = jnp.where(i < n - 1, glu(un_ref[...]), 0.0)
    span = sh_ref.shape[1]
    for j in range(1, SUBLANES):
        sh_ref[j - 1] = z_ref[j:j + span, :]

    for rc in range(tm // chunk):
        acc = jnp.zeros((chunk, CONV_CH), F32)
        for k in range(CONV_WIDTH):
            off = rc * chunk + halo - CONV_HALF + k
            j = off % SUBLANES
            src = z_ref if j == 0 else sh_ref.at[j - 1]
            acc = acc + src[off - j:off - j + chunk, :] * w_ref[k:k + 1, :]
        z = acc + b_ref[...]
        mu = jnp.mean(z, axis=-1, keepdims=True)
        zc = z - mu
        var = jnp.mean(zc * zc, axis=-1, keepdims=True)
        y = zc * lax.rsqrt(var + EPS) * g_ref[...] + beta_ref[...]
        o_ref[rc * chunk:(rc + 1) * chunk, :] = jax.nn.silu(y).astype(o_ref.dtype)


def conv_module(plain, conv_w, conv_b, cnorm_g, cnorm_b, layer, tm=512, halo=16, chunk=64):
    s = plain.shape[0]
    nh = tm // halo
    width = 2 * CONV_CH
    vec_spec = pl.BlockSpec((None, 1, CONV_CH), lambda i: (layer, 0, 0))
    row = lambda v: v.reshape(v.shape[0], 1, CONV_CH)
    return pl.pallas_call(
        functools.partial(_conv_kernel, tm=tm, halo=halo, chunk=chunk),
        grid=(s // tm,),
        in_specs=[pl.BlockSpec((halo, width), lambda i: (jnp.maximum(i * nh - 1, 0), 0)),
                  pl.BlockSpec((tm, width), lambda i: (i, 0)),
                  pl.BlockSpec((halo, width), lambda i: (jnp.minimum((i + 1) * nh, s // halo - 1), 0)),
                  pl.BlockSpec((None, CONV_WIDTH, CONV_CH), lambda i: (layer, 0, 0)),
                  vec_spec, vec_spec, vec_spec],
        out_specs=pl.BlockSpec((tm, CONV_CH), lambda i: (i, 0)),
        out_shape=jax.ShapeDtypeStruct((s, CONV_CH), BF16),
        scratch_shapes=[pltpu.VMEM((tm + 2 * halo, CONV_CH), F32),
                        pltpu.VMEM((SUBLANES - 1, tm + 2 * halo - SUBLANES, CONV_CH), F32)],
        compiler_params=_params(1),
        name="conv_module",
    )(plain, plain, plain, conv_w, row(conv_b), row(cnorm_g), row(cnorm_b))


def _band_tables(bq, half, step):
    shape = (bq, bq + 2 * half)
    qi = lax.broadcasted_iota(jnp.int32, shape, 0)
    kj = lax.broadcasted_iota(jnp.int32, shape, 1)
    rel = jnp.abs(kj - half - qi)
    return rel <= half, (step * rel).astype(F32), kj


def _edge_valid(band, kj, bq, half, blk, n_blocks, first, last):
    valid = band
    if first:
        valid = valid & (kj >= jnp.where(blk > 0, 0, half))
    if last:
        valid = valid & (kj < jnp.where(blk < n_blocks - 1, bq + 2 * half, bq + half))
    return valid


def _scores(q, k):
    return lax.dot_general(q, k, (((1,), (1,)), ((), ())), preferred_element_type=F32)


def _softmax_rows(sc, bias2, valid, sink2):
    t = jnp.where(valid, sc * (HEAD_DIM ** -0.5 * LOG2E) + bias2, NEG * LOG2E)
    m2 = jnp.max(t, axis=-1, keepdims=True)
    if sink2 is not None:
        m2 = jnp.maximum(m2, sink2)
    p = jnp.exp2(t - m2)
    denom = jnp.sum(p, axis=-1, keepdims=True)
    if sink2 is not None:
        denom = denom + jnp.exp2(sink2 - m2)
    return p, m2, denom


def _swa_kernel(slopes_ref, sink_ref, q_ref, kp_ref, kc_ref, kn_ref, vp_ref, vc_ref, vn_ref, o_ref,
                *, tq, half, n_blocks, layer):
    i = pl.program_id(0)
    hk = pl.program_id(1)
    bq = half
    nsub = tq // bq
    kcat = jnp.concatenate([kp_ref[...], kc_ref[...], kn_ref[...]], axis=0)
    vcat = jnp.concatenate([vp_ref[...], vc_ref[...], vn_ref[...]], axis=0)
    band, dist, kj = _band_tables(bq, half, 1)
    heads = [hk * B_GROUP + g for g in range(B_GROUP)]
    biases = [(-LOG2E * slopes_ref[hd]) * dist for hd in heads]
    sinks = [LOG2E * sink_ref[layer, hd] for hd in heads]
    for s in range(nsub):
        rows = slice(s * bq, (s + 1) * bq)
        win = slice(s * bq, (s + 1) * bq + 2 * half)
        valid = _edge_valid(band, kj, bq, half, i * nsub + s, n_blocks, s == 0, s == nsub - 1)
        q_all = jnp.concatenate(
            [q_ref[rows, g * HEAD_DIM:(g + 1) * HEAD_DIM] for g in range(B_GROUP)], axis=0)
        sc_all = _scores(q_all, kcat[win])
        ps, denoms = [], []
        for g in range(B_GROUP):
            p, _, denom = _softmax_rows(sc_all[g * bq:(g + 1) * bq], biases[g], valid, sinks[g])
            ps.append(p.astype(BF16))
            denoms.append(denom)
        o_all = jnp.dot(jnp.concatenate(ps, axis=0), vcat[win], preferred_element_type=F32)
        for g in range(B_GROUP):
            o = o_all[g * bq:(g + 1) * bq] / denoms[g]
            o_ref[rows, g * HEAD_DIM:(g + 1) * HEAD_DIM] = o.astype(o_ref.dtype)


def windowed_gqa(plain, slopes, sink, layer, tq=1024):
    s = plain.shape[0]
    half = B_HALF_WINDOW
    nh = tq // half
    qw = B_GROUP * HEAD_DIM
    q_blk0 = 2 * CONV_CH // qw
    k_blk0 = (2 * CONV_CH + B_Q_HEADS * HEAD_DIM) // HEAD_DIM
    v_blk0 = k_blk0 + B_KV_HEADS

    def halo_specs(blk0):
        return [pl.BlockSpec((half, HEAD_DIM), lambda i, h: (jnp.maximum(i * nh - 1, 0), blk0 + h)),
                pl.BlockSpec((tq, HEAD_DIM), lambda i, h: (i, blk0 + h)),
                pl.BlockSpec((half, HEAD_DIM),
                             lambda i, h: (jnp.minimum((i + 1) * nh, s // half - 1), blk0 + h))]

    smem = pl.BlockSpec(memory_space=pltpu.SMEM)
    return pl.pallas_call(
        functools.partial(_swa_kernel, tq=tq, half=half, n_blocks=s // half, layer=layer),
        grid=(s // tq, B_KV_HEADS),
        in_specs=[smem, smem, pl.BlockSpec((tq, qw), lambda i, h: (i, q_blk0 + h))]
        + halo_specs(k_blk0) + halo_specs(v_blk0),
        out_specs=pl.BlockSpec((tq, qw), lambda i, h: (i, h)),
        out_shape=jax.ShapeDtypeStruct((s, B_Q_HEADS * HEAD_DIM), BF16),
        compiler_params=_params(2),
        name="windowed_gqa",
    )(slopes, sink, plain, plain, plain, plain, plain, plain, plain)


def _dilated_kernel(slopes_ref, *refs, tokens, bq, half, seq_len):
    o_ref, o_scr, l_scr = refs[-3:]
    i = pl.program_id(0)
    h = pl.program_id(1)
    for gi, (_, r) in enumerate(C_PATTERNS):
        q_ref, kp_ref, kc_ref, kn_ref, vp_ref, vc_ref, vn_ref = refs[7 * gi:7 * gi + 7]
        rows = tokens // r
        nsub = rows // bq
        n_blocks = seq_len // r // bq
        band, dist, kj = _band_tables(bq, half, r)
        bias = (-LOG2E * slopes_ref[B_Q_HEADS + gi * C_HEADS_PER_GROUP + h]) * dist
        for c in range(r):
            kcat = jnp.concatenate([kp_ref[c], kc_ref[c], kn_ref[c]], axis=0)
            vcat = jnp.concatenate([vp_ref[c], vc_ref[c], vn_ref[c]], axis=0)
            for s in range(nsub):
                win = slice(s * bq, (s + 1) * bq + 2 * half)
                valid = _edge_valid(band, kj, bq, half, i * nsub + s, n_blocks, s == 0, s == nsub - 1)
                p, m2, denom = _softmax_rows(_scores(q_ref[c, s * bq:(s + 1) * bq, :], kcat[win]),
                                             bias, valid, None)
                o = jnp.dot(p.astype(BF16), vcat[win], preferred_element_type=F32) / denom
                lse = m2 * LN2 + jnp.log(denom)
                dst = pl.ds(s * bq * r + c, bq, stride=r) if r > 1 else pl.ds(s * bq, bq)
                o_scr[gi, dst, :] = o
                l_scr[gi, dst, :] = jnp.broadcast_to(lse, (bq, HEAD_DIM))
    l0, l1, l2 = l_scr[0], l_scr[1], l_scr[2]
    m = jnp.maximum(jnp.maximum(l0, l1), l2)
    e0, e1, e2 = jnp.exp(l0 - m), jnp.exp(l1 - m), jnp.exp(l2 - m)
    out = (e0 * o_scr[0] + e1 * o_scr[1] + e2 * o_scr[2]) / (e0 + e1 + e2)
    o_ref[...] = out.astype(o_ref.dtype)


def dilated_attention(lattices, slopes, seq_len, tokens=C_TOKENS, bq=C_BLOCK_Q, half=C_HALF):
    nq = C_HEADS_PER_GROUP
    in_specs = [pl.BlockSpec(memory_space=pltpu.SMEM)]
    args = [slopes]
    for (_, r), lat in zip(C_PATTERNS, lattices):
        rows = tokens // r
        nh = rows // half
        last = seq_len // r // half - 1

        def specs(blk0, r=r, rows=rows, nh=nh, last=last):
            return [pl.BlockSpec((r, half, HEAD_DIM), lambda i, h: (0, jnp.maximum(i * nh - 1, 0), blk0 + h)),
                    pl.BlockSpec((r, rows, HEAD_DIM), lambda i, h: (0, i, blk0 + h)),
                    pl.BlockSpec((r, half, HEAD_DIM), lambda i, h: (0, jnp.minimum((i + 1) * nh, last), blk0 + h))]

        in_specs += [pl.BlockSpec((r, rows, HEAD_DIM), lambda i, h: (0, i, h))] + specs(nq) + specs(2 * nq)
        args += [lat] * 7
    return pl.pallas_call(
        functools.partial(_dilated_kernel, tokens=tokens, bq=bq, half=half, seq_len=seq_len),
        grid=(seq_len // tokens, nq),
        in_specs=in_specs,
        out_specs=pl.BlockSpec((tokens, HEAD_DIM), lambda i, h: (i, h)),
        out_shape=jax.ShapeDtypeStruct((seq_len, nq * HEAD_DIM), BF16),
        scratch_shapes=[pltpu.VMEM((len(C_PATTERNS), tokens, HEAD_DIM), F32),
                        pltpu.VMEM((len(C_PATTERNS), tokens, HEAD_DIM), F32)],
        compiler_params=_params(2),
        name="dilated_attention",
    )(*args)


def _alibi_slopes():
    h = jnp.arange(1, N_ATTN_HEADS + 1, dtype=F32)
    return jnp.exp2(-ALIBI_MAX_EXP * h / N_ATTN_HEADS)


def kernel(x, ln1_g, w_in, conv_w, conv_b, cnorm_g, cnorm_b, w_a, sink, w_b, w_c, w_o, ln2_g,
           w_ffn_in, w_ffn_out, lnf_g):
    b, s, d = x.shape
    depth = ln1_g.shape[0]
    slopes = _alibi_slopes()
    outs = []
    for xb in jnp.split(x, b, axis=0):
        xb = xb.reshape(s, d)
        for l in range(depth):
            h = rmsnorm(xb, ln1_g[l], BF16)
            plain = proj_plain(h, w_in, l)
            lattices = [proj_lattice(h, w_in, l, gi, r) for gi, (_, r) in enumerate(C_PATTERNS)]
            ca = conv_module(plain, conv_w, conv_b, cnorm_g, cnorm_b, l)
            ob = windowed_gqa(plain, slopes, sink, l)
            oc = dilated_attention(lattices, slopes, s)
            mixed = gated_merge(h, ca, ob, oc, w_in, w_a, w_b, w_c, l)
            xb, h2 = out_proj_norm(mixed, w_o, l, xb, ln2_g[l])
            act = swiglu_in(h2, w_ffn_in, l)
            xb = matmul_residual(act, w_ffn_out[l].astype(BF16), xb)
        outs.append(rmsnorm(xb, lnf_g, x.dtype).reshape(1, s, d))
    return outs[0] if b == 1 else jnp.concatenate(outs, axis=0)
```

```python
import functools

import jax
import jax.numpy as jnp
from jax import lax
from jax.experimental import pallas as pl
from jax.experimental.pallas import tpu as pltpu

F32 = jnp.float32
BF16 = jnp.bfloat16

D_MODEL = 2048
HEAD_DIM = 128
CONV_CH = 512
CONV_WIDTH = 31
CONV_HALF = CONV_WIDTH // 2
B_Q_HEADS = 8
B_KV_HEADS = 2
B_GROUP = B_Q_HEADS // B_KV_HEADS
B_HALF_WINDOW = 128
C_PATTERNS = ((128, 1), (512, 4), (2048, 16))
C_HEADS_PER_GROUP = 4
C_HALF = 64
N_BRANCHES = 3
N_ATTN_HEADS = B_Q_HEADS + len(C_PATTERNS) * C_HEADS_PER_GROUP
ALIBI_MAX_EXP = 8.0
D_FF = -(-8 * D_MODEL // (3 * 256)) * 256
EPS = 1e-6
NEG = -1e30
LOG2E = 1.4426950408889634

LANES = 128
MXU_COLS = 256
SUBLANES = 8
LATTICE_STRIDE = 4
COL = 512
PLAIN_COLS = 5 * COL
QC_BLOCK0 = 5
GATE_BLOCK0 = 14
CONV_TILE = 512
CONV_HALO = 16
C_TOKENS = 2048
C_BLOCK_Q = 128

VMEM_LIMIT = 56 * 1024 * 1024


def _params(n_axes, flags=None):
    return pltpu.CompilerParams(
        dimension_semantics=("arbitrary",) * n_axes, vmem_limit_bytes=VMEM_LIMIT, flags=flags)


def _cast_rows(src_ref, dst_ref, chunk=256):
    rows = src_ref.shape[0]
    chunk = min(chunk, rows)

    def body(c, carry):
        sl = pl.ds(pl.multiple_of(c * chunk, chunk), chunk)
        dst_ref[sl, :] = src_ref[sl, :].astype(dst_ref.dtype)
        return carry

    lax.fori_loop(0, rows // chunk, body, 0)


def _rms(x, g):
    ms = jnp.mean(x * x, axis=-1, keepdims=True)
    return x * lax.rsqrt(ms + EPS) * g


def _rmsnorm_kernel(x_ref, g_ref, o_ref):
    o_ref[...] = _rms(x_ref[...], g_ref[...]).astype(o_ref.dtype)


def rmsnorm(x, g, out_dtype, tm=512):
    s, d = x.shape
    return pl.pallas_call(
        _rmsnorm_kernel,
        grid=(s // tm,),
        in_specs=[pl.BlockSpec((tm, d), lambda i: (i, 0)),
                  pl.BlockSpec((1, d), lambda i: (0, 0))],
        out_specs=pl.BlockSpec((tm, d), lambda i: (i, 0)),
        out_shape=jax.ShapeDtypeStruct((s, d), out_dtype),
        compiler_params=_params(1),
        name="rmsnorm",
    )(x, g.reshape(1, d))


def _ws_mm_kernel(a_ref, w_ref, o_ref, w16_ref):
    @pl.when(pl.program_id(1) == 0)
    def _():
        _cast_rows(w_ref, w16_ref)

    o_ref[...] = jnp.dot(a_ref[...], w16_ref[...],
                         preferred_element_type=F32).astype(o_ref.dtype)


def proj_plain(h, w_in, layer, tm=1024, tn=PLAIN_COLS // 2):
    s, d = h.shape
    return pl.pallas_call(
        _ws_mm_kernel,
        grid=(PLAIN_COLS // tn, s // tm),
        in_specs=[pl.BlockSpec((tm, d), lambda n, m: (m, 0)),
                  pl.BlockSpec((None, d, tn), lambda n, m: (layer, 0, n))],
        out_specs=pl.BlockSpec((tm, tn), lambda n, m: (m, n)),
        out_shape=jax.ShapeDtypeStruct((s, PLAIN_COLS), BF16),
        scratch_shapes=[pltpu.VMEM((d, tn), BF16)],
        compiler_params=_params(2),
        name="proj_plain",
    )(h, w_in)


def _lattice_kernel(a_ref, wq_ref, wk_ref, wv_ref, o_ref, w16_ref, acc_ref, tmp_ref, *, r):
    @pl.when(pl.program_id(0) == 0)
    def _():
        for j, w_ref in enumerate((wq_ref, wk_ref, wv_ref)):
            _cast_rows(w_ref, w16_ref.at[j])

    a = a_ref[...]
    rows = a.shape[0] // r
    slabs = COL // LANES
    for j in range(3):
        acc = jnp.dot(a, w16_ref[j], preferred_element_type=F32)
        if r == 1:
            o_ref[0, :, j * COL:(j + 1) * COL] = acc.astype(o_ref.dtype)
            continue
        for s in range(slabs):
            acc_ref[j * slabs + s] = acc[:, s * LANES:(s + 1) * LANES]
        for s in range(slabs):
            slab = acc_ref.at[j * slabs + s]
            col = j * COL + s * LANES
            if r == LATTICE_STRIDE:
                for c in range(r):
                    o_ref[c, :, col:col + LANES] = slab[pl.ds(c, rows, stride=r), :].astype(o_ref.dtype)
                continue
            assert r == LATTICE_STRIDE ** 2
            tmp = tmp_ref.at[j * slabs + s]
            for c0 in range(LATTICE_STRIDE):
                tmp[c0] = slab[pl.ds(c0, rows * LATTICE_STRIDE, stride=LATTICE_STRIDE), :]
            for c0 in range(LATTICE_STRIDE):
                for c1 in range(LATTICE_STRIDE):
                    o_ref[c0 + LATTICE_STRIDE * c1, :, col:col + LANES] = (
                        tmp[c0, pl.ds(c1, rows, stride=LATTICE_STRIDE), :].astype(o_ref.dtype))


def proj_lattice(h, w_in, layer, group, r, tm=1024):
    s, d = h.shape
    n_slabs = 3 * COL // LANES
    acc_rows = tm if r > 1 else SUBLANES
    two_pass_rows = tm // LATTICE_STRIDE if r == LATTICE_STRIDE ** 2 else SUBLANES

    def w_spec(j):
        return pl.BlockSpec((None, d, COL), lambda m: (layer, 0, QC_BLOCK0 + group + 3 * j),
                            pipeline_mode=pl.Buffered(1))

    return pl.pallas_call(
        functools.partial(_lattice_kernel, r=r),
        grid=(s // tm,),
        in_specs=[pl.BlockSpec((tm, d), lambda m: (m, 0)), w_spec(0), w_spec(1), w_spec(2)],
        out_specs=pl.BlockSpec((r, tm // r, 3 * COL), lambda m: (0, m, 0)),
        out_shape=jax.ShapeDtypeStruct((r, s // r, 3 * COL), BF16),
        scratch_shapes=[pltpu.VMEM((3, d, COL), BF16),
                        pltpu.VMEM((n_slabs, acc_rows, LANES), F32),
                        pltpu.VMEM((n_slabs, LATTICE_STRIDE, two_pass_rows, LANES), F32)],
        compiler_params=_params(1),
        name=f"proj_lattice_r{r}",
    )(h, w_in, w_in, w_in)


def _merge_kernel(h_ref, ca_ref, ob_ref, oc_ref, wg0_ref, wg1_ref, wg2_ref,
                  wa_ref, wb_ref, wc_ref, o_ref, wg16_ref, wa16_ref, wb16_ref, wc16_ref):
    @pl.when(pl.program_id(1) == 0)
    def _():
        for b, wg_ref in enumerate((wg0_ref, wg1_ref, wg2_ref)):
            _cast_rows(wg_ref, wg16_ref.at[b])
        _cast_rows(wa_ref, wa16_ref)
        _cast_rows(wb_ref, wb16_ref)
        _cast_rows(wc_ref, wc16_ref)

    h = h_ref[...]
    ys = (ca_ref[...], ob_ref[...], oc_ref[...])
    ws = (wa16_ref, wb16_ref, wc16_ref)
    for c0 in range(0, o_ref.shape[1], MXU_COLS):
        cols = slice(c0, c0 + MXU_COLS)
        mixed = None
        for b in range(N_BRANCHES):
            gate = jax.nn.sigmoid(jnp.dot(h, wg16_ref[b, :, cols], preferred_element_type=F32))
            term = gate * jnp.dot(ys[b], ws[b][:, cols], preferred_element_type=F32)
            mixed = term if mixed is None else mixed + term
        o_ref[:, cols] = mixed.astype(o_ref.dtype)


def gated_merge(h, ca, ob, oc, w_in, w_a, w_b, w_c, layer, tm=1024, tn=COL):
    s, d = h.shape
    nj = d // tn
    widths = (ca.shape[1], ob.shape[1], oc.shape[1])

    def gate_spec(b):
        return pl.BlockSpec((None, d, tn), lambda n, m: (layer, 0, GATE_BLOCK0 + b * nj + n),
                            pipeline_mode=pl.Buffered(1))

    def lhs_spec(width):
        return pl.BlockSpec((tm, width), lambda n, m: (m, 0))

    def w_spec(width):
        return pl.BlockSpec((None, width, tn), lambda n, m: (layer, 0, n),
                            pipeline_mode=pl.Buffered(1))

    return pl.pallas_call(
        _merge_kernel,
        grid=(nj, s // tm),
        in_specs=[lhs_spec(d)] + [lhs_spec(w) for w in widths]
        + [gate_spec(0), gate_spec(1), gate_spec(2)] + [w_spec(w) for w in widths],
        out_specs=pl.BlockSpec((tm, tn), lambda n, m: (m, n)),
        out_shape=jax.ShapeDtypeStruct((s, d), BF16),
        scratch_shapes=[pltpu.VMEM((N_BRANCHES, d, tn), BF16)]
        + [pltpu.VMEM((w, tn), BF16) for w in widths],
        compiler_params=_params(2),
        name="gated_merge",
    )(h, ca, ob, oc, w_in, w_in, w_in, w_a, w_b, w_c)


def _out_proj_kernel(a_ref, w_ref, x_ref, g_ref, xo_ref, ho_ref, w16_ref):
    @pl.when(pl.program_id(0) == 0)
    def _():
        _cast_rows(w_ref, w16_ref)

    x = x_ref[...] + jnp.dot(a_ref[...], w16_ref[...], preferred_element_type=F32)
    xo_ref[...] = x
    ho_ref[...] = _rms(x, g_ref[...]).astype(ho_ref.dtype)


def out_proj_norm(a, w_o, layer, x, g, tm=512):
    s, k = a.shape
    d = x.shape[1]
    row = pl.BlockSpec((tm, d), lambda m: (m, 0))
    return pl.pallas_call(
        _out_proj_kernel,
        grid=(s // tm,),
        in_specs=[pl.BlockSpec((tm, k), lambda m: (m, 0)),
                  pl.BlockSpec((None, k, d), lambda m: (layer, 0, 0), pipeline_mode=pl.Buffered(1)),
                  row,
                  pl.BlockSpec((1, d), lambda m: (0, 0))],
        out_specs=[row, row],
        out_shape=[jax.ShapeDtypeStruct((s, d), F32), jax.ShapeDtypeStruct((s, d), BF16)],
        scratch_shapes=[pltpu.VMEM((k, d), BF16)],
        compiler_params=_params(1),
        name="out_proj_norm",
    )(a, w_o, x, g.reshape(1, d))


def _swiglu_kernel(h_ref, wg_ref, wu_ref, o_ref, w16_ref):
    @pl.when(pl.program_id(1) == 0)
    def _():
        _cast_rows(wg_ref, w16_ref.at[0])
        _cast_rows(wu_ref, w16_ref.at[1])

    h = h_ref[...]
    for c0 in range(0, o_ref.shape[1], MXU_COLS):
        cols = slice(c0, c0 + MXU_COLS)
        g = jnp.dot(h, w16_ref[0, :, cols], preferred_element_type=F32)
        u = jnp.dot(h, w16_ref[1, :, cols], preferred_element_type=F32)
        o_ref[:, cols] = (jax.nn.silu(g) * u).astype(o_ref.dtype)


def swiglu_in(h, w_ffn_in, layer, tm=2048, tn=COL):
    s, d = h.shape
    nj = D_FF // tn
    return pl.pallas_call(
        _swiglu_kernel,
        grid=(nj, s // tm),
        in_specs=[pl.BlockSpec((tm, d), lambda n, m: (m, 0)),
                  pl.BlockSpec((None, d, tn), lambda n, m: (layer, 0, n)),
                  pl.BlockSpec((None, d, tn), lambda n, m: (layer, 0, nj + n))],
        out_specs=pl.BlockSpec((tm, tn), lambda n, m: (m, n)),
        out_shape=jax.ShapeDtypeStruct((s, D_FF), BF16),
        scratch_shapes=[pltpu.VMEM((2, d, tn), BF16)],
        compiler_params=_params(2),
        name="swiglu_in",
    )(h, w_ffn_in, w_ffn_in)


def _mm_residual_kernel(a_ref, w_ref, x_ref, o_ref):
    o_ref[...] = x_ref[...] + jnp.dot(a_ref[...], w_ref[...], preferred_element_type=F32)


def matmul_residual(a, w16, x, tm=1024, tn=COL):
    s, k = a.shape
    d = w16.shape[1]
    return pl.pallas_call(
        _mm_residual_kernel,
        grid=(s // tm, d // tn),
        in_specs=[pl.BlockSpec((tm, k), lambda i, j: (i, 0)),
                  pl.BlockSpec((k, tn), lambda i, j: (0, j)),
                  pl.BlockSpec((tm, tn), lambda i, j: (i, j))],
        out_specs=pl.BlockSpec((tm, tn), lambda i, j: (i, j)),
        out_shape=jax.ShapeDtypeStruct((s, d), F32),
        compiler_params=_params(2),
        name="matmul_residual",
    )(a, w16, x)


def _conv_tile(up_ref, uc_ref, un_ref, w_ref, b_ref, g_ref, beta_ref, o_ref, z_ref, sh_ref,
               *, first, last, chunk=64):
    tm, halo = CONV_TILE, CONV_HALO

    def glu(u):
        u = u.astype(F32)
        return u[:, :CONV_CH] * jax.nn.sigmoid(u[:, CONV_CH:])

    z_ref[0:halo] = jnp.where(first, 0.0, glu(up_ref[...]))
    z_ref[halo:halo + tm] = glu(uc_ref[...])
    z_ref[halo + tm:2 * halo + tm] = jnp.where(last, 0.0, glu(un_ref[...]))
    span = sh_ref.shape[1]
    for j in range(1, SUBLANES):
        sh_ref[j - 1] = z_ref[j:j + span, :]

    for rc in range(tm // chunk):
        acc = jnp.zeros((chunk, CONV_CH), F32)
        for k in range(CONV_WIDTH):
            off = rc * chunk + halo - CONV_HALF + k
            j = off % SUBLANES
            src = z_ref if j == 0 else sh_ref.at[j - 1]
            acc = acc + src[off - j:off - j + chunk, :] * w_ref[k:k + 1, :]
        z = acc + b_ref[...]
        mu = jnp.mean(z, axis=-1, keepdims=True)
        zc = z - mu
        var = jnp.mean(zc * zc, axis=-1, keepdims=True)
        y = zc * lax.rsqrt(var + EPS) * g_ref[...] + beta_ref[...]
        o_ref[rc * chunk:(rc + 1) * chunk, :] = jax.nn.silu(y).astype(o_ref.dtype)


def _conv_kernel(*refs):
    i = pl.program_id(0)
    _conv_tile(*refs, first=i == 0, last=i == pl.num_programs(0) - 1)


def conv_module(plain, conv_w, conv_b, cnorm_g, cnorm_b, layer):
    s = plain.shape[0]
    nh = CONV_TILE // CONV_HALO
    width = 2 * CONV_CH
    vec_spec = pl.BlockSpec((None, 1, CONV_CH), lambda i: (layer, 0, 0))
    row = lambda v: v.reshape(v.shape[0], 1, CONV_CH)
    return pl.pallas_call(
        _conv_kernel,
        grid=(s // CONV_TILE,),
        in_specs=[pl.BlockSpec((CONV_HALO, width), lambda i: (jnp.maximum(i * nh - 1, 0), 0)),
                  pl.BlockSpec((CONV_TILE, width), lambda i: (i, 0)),
                  pl.BlockSpec((CONV_HALO, width),
                               lambda i: (jnp.minimum((i + 1) * nh, s // CONV_HALO - 1), 0)),
                  pl.BlockSpec((None, CONV_WIDTH, CONV_CH), lambda i: (layer, 0, 0)),
                  vec_spec, vec_spec, vec_spec],
        out_specs=pl.BlockSpec((CONV_TILE, CONV_CH), lambda i: (i, 0)),
        out_shape=jax.ShapeDtypeStruct((s, CONV_CH), BF16),
        scratch_shapes=[pltpu.VMEM((CONV_TILE + 2 * CONV_HALO, CONV_CH), F32),
                        pltpu.VMEM((SUBLANES - 1, CONV_TILE + 2 * CONV_HALO - SUBLANES, CONV_CH), F32)],
        compiler_params=_params(1),
        name="conv_module",
    )(plain, plain, plain, conv_w, row(conv_b), row(cnorm_g), row(cnorm_b))


def _band_tables(bq, half, step):
    shape = (bq, bq + 2 * half)
    qi = lax.broadcasted_iota(jnp.int32, shape, 0)
    kj = lax.broadcasted_iota(jnp.int32, shape, 1)
    rel = jnp.abs(kj - half - qi)
    return rel <= half, (step * rel).astype(F32), kj


def _edge_valid(band, kj, bq, half, blk, n_blocks, first, last):
    valid = band
    if first:
        valid = valid & (kj >= jnp.where(blk > 0, 0, half))
    if last:
        valid = valid & (kj < jnp.where(blk < n_blocks - 1, bq + 2 * half, bq + half))
    return valid


def _scores(q, k):
    return lax.dot_general(q, k, (((1,), (1,)), ((), ())), preferred_element_type=F32)


def _softmax_rows(sc, bias2, valid, sink2):
    t = jnp.where(valid, sc * (HEAD_DIM ** -0.5 * LOG2E) + bias2, NEG * LOG2E)
    m2 = jnp.max(t, axis=-1, keepdims=True)
    if sink2 is not None:
        m2 = jnp.maximum(m2, sink2)
    p = jnp.exp2(t - m2)
    denom = jnp.sum(p, axis=-1, keepdims=True)
    if sink2 is not None:
        denom = denom + jnp.exp2(sink2 - m2)
    return p, m2, denom


def _swa_kernel(slopes_ref, sink_ref, q_ref, kp_ref, kc_ref, kn_ref, vp_ref, vc_ref, vn_ref, o_ref,
                *, tq, half, n_blocks, layer):
    i = pl.program_id(0)
    hk = pl.program_id(1)
    bq = half
    nsub = tq // bq
    kcat = jnp.concatenate([kp_ref[...], kc_ref[...], kn_ref[...]], axis=0)
    vcat = jnp.concatenate([vp_ref[...], vc_ref[...], vn_ref[...]], axis=0)
    band, dist, kj = _band_tables(bq, half, 1)
    heads = [hk * B_GROUP + g for g in range(B_GROUP)]
    biases = [(-LOG2E * slopes_ref[hd]) * dist for hd in heads]
    sinks = [LOG2E * sink_ref[layer, hd] for hd in heads]
    for s in range(nsub):
        rows = slice(s * bq, (s + 1) * bq)
        win = slice(s * bq, (s + 1) * bq + 2 * half)
        valid = _edge_valid(band, kj, bq, half, i * nsub + s, n_blocks, s == 0, s == nsub - 1)
        q_all = jnp.concatenate(
            [q_ref[rows, g * HEAD_DIM:(g + 1) * HEAD_DIM] for g in range(B_GROUP)], axis=0)
        sc_all = _scores(q_all, kcat[win])
        ps, denoms = [], []
        for g in range(B_GROUP):
            p, _, denom = _softmax_rows(sc_all[g * bq:(g + 1) * bq], biases[g], valid, sinks[g])
            ps.append(p.astype(BF16))
            denoms.append(denom)
        o_all = jnp.dot(jnp.concatenate(ps, axis=0), vcat[win], preferred_element_type=F32)
        for g in range(B_GROUP):
            o = o_all[g * bq:(g + 1) * bq] / denoms[g]
            o_ref[rows, g * HEAD_DIM:(g + 1) * HEAD_DIM] = o.astype(o_ref.dtype)


def windowed_gqa(plain, slopes, sink, layer, tq=1024):
    s = plain.shape[0]
    half = B_HALF_WINDOW
    nh = tq // half
    qw = B_GROUP * HEAD_DIM
    q_blk0 = 2 * CONV_CH // qw
    k_blk0 = (2 * CONV_CH + B_Q_HEADS * HEAD_DIM) // HEAD_DIM
    v_blk0 = k_blk0 + B_KV_HEADS

    def halo_specs(blk0):
        return [pl.BlockSpec((half, HEAD_DIM), lambda i, h: (jnp.maximum(i * nh - 1, 0), blk0 + h)),
                pl.BlockSpec((tq, HEAD_DIM), lambda i, h: (i, blk0 + h)),
                pl.BlockSpec((half, HEAD_DIM),
                             lambda i, h: (jnp.minimum((i + 1) * nh, s // half - 1), blk0 + h))]

    smem = pl.BlockSpec(memory_space=pltpu.SMEM)
    return pl.pallas_call(
        functools.partial(_swa_kernel, tq=tq, half=half, n_blocks=s // half, layer=layer),
        grid=(s // tq, B_KV_HEADS),
        in_specs=[smem, smem, pl.BlockSpec((tq, qw), lambda i, h: (i, q_blk0 + h))]
        + halo_specs(k_blk0) + halo_specs(v_blk0),
        out_specs=pl.BlockSpec((tq, qw), lambda i, h: (i, h)),
        out_shape=jax.ShapeDtypeStruct((s, B_Q_HEADS * HEAD_DIM), BF16),
        compiler_params=_params(2),
        name="windowed_gqa",
    )(slopes, sink, plain, plain, plain, plain, plain, plain, plain)


def _dilated_kernel(slopes_ref, *refs, tokens, bq, half, seq_len):
    o_ref, u_scr, m_scr, d_scr = refs[-4:]
    i = pl.program_id(0)
    h = pl.program_id(1)
    for gi, (_, r) in enumerate(C_PATTERNS):
        q_ref, kp_ref, kc_ref, kn_ref, vp_ref, vc_ref, vn_ref = refs[7 * gi:7 * gi + 7]
        rows = tokens // r
        nsub = rows // bq
        n_blocks = seq_len // r // bq
        band, dist, kj = _band_tables(bq, half, r)
        bias = (-LOG2E * slopes_ref[B_Q_HEADS + gi * C_HEADS_PER_GROUP + h]) * dist
        for c in range(r):
            kcat = jnp.concatenate([kp_ref[c], kc_ref[c], kn_ref[c]], axis=0)
            vcat = jnp.concatenate([vp_ref[c], vc_ref[c], vn_ref[c]], axis=0)
            for s in range(nsub):
                win = slice(s * bq, (s + 1) * bq + 2 * half)
                valid = _edge_valid(band, kj, bq, half, i * nsub + s, n_blocks, s == 0, s == nsub - 1)
                p, m2, denom = _softmax_rows(_scores(q_ref[c, s * bq:(s + 1) * bq, :], kcat[win]),
                                             bias, valid, None)
                dst = pl.ds(s * bq * r + c, bq, stride=r) if r > 1 else pl.ds(s * bq, bq)
                u_scr[gi, dst, :] = jnp.dot(p.astype(BF16), vcat[win], preferred_element_type=F32)
                m_scr[gi, dst, :] = jnp.broadcast_to(m2, (bq, HEAD_DIM))
                d_scr[gi, dst, :] = jnp.broadcast_to(denom, (bq, HEAD_DIM))
    m_max = jnp.maximum(jnp.maximum(m_scr[0], m_scr[1]), m_scr[2])
    num = den = None
    for gi in range(len(C_PATTERNS)):
        w = jnp.exp2(m_scr[gi] - m_max)
        num = w * u_scr[gi] if num is None else num + w * u_scr[gi]
        den = w * d_scr[gi] if den is None else den + w * d_scr[gi]
    o_ref[...] = (num / den).astype(o_ref.dtype)


def dilated_attention(lattices, slopes, seq_len, tokens=C_TOKENS, bq=C_BLOCK_Q, half=C_HALF):
    nq = C_HEADS_PER_GROUP
    in_specs = [pl.BlockSpec(memory_space=pltpu.SMEM)]
    args = [slopes]
    for (_, r), lat in zip(C_PATTERNS, lattices):
        rows = tokens // r
        nh = rows // half
        last = seq_len // r // half - 1

        def specs(blk0, r=r, rows=rows, nh=nh, last=last):
            return [pl.BlockSpec((r, half, HEAD_DIM), lambda i, h: (0, jnp.maximum(i * nh - 1, 0), blk0 + h)),
                    pl.BlockSpec((r, rows, HEAD_DIM), lambda i, h: (0, i, blk0 + h)),
                    pl.BlockSpec((r, half, HEAD_DIM), lambda i, h: (0, jnp.minimum((i + 1) * nh, last), blk0 + h))]

        in_specs += [pl.BlockSpec((r, rows, HEAD_DIM), lambda i, h: (0, i, h))] + specs(nq) + specs(2 * nq)
        args += [lat] * 7
    return pl.pallas_call(
        functools.partial(_dilated_kernel, tokens=tokens, bq=bq, half=half, seq_len=seq_len),
        grid=(seq_len // tokens, nq),
        in_specs=in_specs,
        out_specs=pl.BlockSpec((tokens, HEAD_DIM), lambda i, h: (i, h)),
        out_shape=jax.ShapeDtypeStruct((seq_len, nq * HEAD_DIM), BF16),
        scratch_shapes=[pltpu.VMEM((len(C_PATTERNS), tokens, HEAD_DIM), F32)] * 3,
        compiler_params=_params(2),
        name="dilated_attention",
    )(*args)


def _alibi_slopes():
    h = jnp.arange(1, N_ATTN_HEADS + 1, dtype=F32)
    return jnp.exp2(-ALIBI_MAX_EXP * h / N_ATTN_HEADS)


def kernel(x, ln1_g, w_in, conv_w, conv_b, cnorm_g, cnorm_b, w_a, sink, w_b, w_c, w_o, ln2_g,
           w_ffn_in, w_ffn_out, lnf_g):
    b, s, d = x.shape
    depth = ln1_g.shape[0]
    slopes = _alibi_slopes()
    outs = []
    for xb in jnp.split(x, b, axis=0):
        xb = xb.reshape(s, d)
        for l in range(depth):
            h = rmsnorm(xb, ln1_g[l], BF16)
            plain = proj_plain(h, w_in, l)
            lattices = [proj_lattice(h, w_in, l, gi, r) for gi, (_, r) in enumerate(C_PATTERNS)]
            ca = conv_module(plain, conv_w, conv_b, cnorm_g, cnorm_b, l)
            ob = windowed_gqa(plain, slopes, sink, l)
            oc = dilated_attention(lattices, slopes, s)
            mixed = gated_merge(h, ca, ob, oc, w_in, w_a, w_b, w_c, l)
            xb, h2 = out_proj_norm(mixed, w_o, l, xb, ln2_g[l])
            act = swiglu_in(h2, w_ffn_in, l)
            xb = matmul_residual(act, w_ffn_out[l].astype(BF16), xb)
        outs.append(rmsnorm(xb, lnf_g, x.dtype).reshape(1, s, d))
    return outs[0] if b == 1 else jnp.concatenate(outs, axis=0)
```

```python
import functools

import jax
import jax.numpy as jnp
from jax import lax
from jax.experimental import pallas as pl
from jax.experimental.pallas import tpu as pltpu

F32 = jnp.float32
BF16 = jnp.bfloat16

D_MODEL = 2048
HEAD_DIM = 128
CONV_CH = 512
CONV_WIDTH = 31
CONV_HALF = CONV_WIDTH // 2
B_Q_HEADS = 8
B_KV_HEADS = 2
B_GROUP = B_Q_HEADS // B_KV_HEADS
B_HALF_WINDOW = 128
C_PATTERNS = ((128, 1), (512, 4), (2048, 16))
C_HEADS_PER_GROUP = 4
C_HALF = 64
assert C_PATTERNS[0][1] == 1 and all(w // (2 * r) == C_HALF for w, r in C_PATTERNS)
N_BRANCHES = 3
N_ATTN_HEADS = B_Q_HEADS + len(C_PATTERNS) * C_HEADS_PER_GROUP
ALIBI_MAX_EXP = 8.0
D_FF = -(-8 * D_MODEL // (3 * 256)) * 256
EPS = 1e-6
NEG = -1e30
LOG2E = 1.4426950408889634

LANES = 128
MXU_COLS = 256
SUBLANES = 8
LATTICE_STRIDE = 4
COL = 512
PLAIN_COLS = 5 * COL
QC_BLOCK0 = 5
GATE_BLOCK0 = 14
CONV_TILE = 512
CONV_HALO = 16
C_TOKENS = 2048
C_BLOCK_Q = 128

VMEM_LIMIT = 56 * 1024 * 1024


def _params(n_axes, flags=None):
    return pltpu.CompilerParams(
        dimension_semantics=("arbitrary",) * n_axes, vmem_limit_bytes=VMEM_LIMIT, flags=flags)


def _cast_rows(src_ref, dst_ref, chunk=256):
    rows = src_ref.shape[0]
    chunk = min(chunk, rows)

    def body(c, carry):
        sl = pl.ds(pl.multiple_of(c * chunk, chunk), chunk)
        dst_ref[sl, :] = src_ref[sl, :].astype(dst_ref.dtype)
        return carry

    lax.fori_loop(0, rows // chunk, body, 0)


def _rms(x, g):
    ms = jnp.mean(x * x, axis=-1, keepdims=True)
    return x * lax.rsqrt(ms + EPS) * g


def _rmsnorm_kernel(x_ref, g_ref, o_ref):
    o_ref[...] = _rms(x_ref[...], g_ref[...]).astype(o_ref.dtype)


def rmsnorm(x, g, out_dtype, tm=1024):
    s, d = x.shape
    return pl.pallas_call(
        _rmsnorm_kernel,
        grid=(s // tm,),
        in_specs=[pl.BlockSpec((tm, d), lambda i: (i, 0)),
                  pl.BlockSpec((1, d), lambda i: (0, 0))],
        out_specs=pl.BlockSpec((tm, d), lambda i: (i, 0)),
        out_shape=jax.ShapeDtypeStruct((s, d), out_dtype),
        compiler_params=_params(1),
        name="rmsnorm",
    )(x, g.reshape(1, d))


def _ws_mm_kernel(a_ref, w_ref, o_ref, w16_ref):
    @pl.when(pl.program_id(1) == 0)
    def _():
        _cast_rows(w_ref, w16_ref)

    o_ref[...] = jnp.dot(a_ref[...], w16_ref[...],
                         preferred_element_type=F32).astype(o_ref.dtype)


def proj_plain(h, w_in, layer, tm=1024, tn=PLAIN_COLS // 2):
    s, d = h.shape
    return pl.pallas_call(
        _ws_mm_kernel,
        grid=(PLAIN_COLS // tn, s // tm),
        in_specs=[pl.BlockSpec((tm, d), lambda n, m: (m, 0)),
                  pl.BlockSpec((None, d, tn), lambda n, m: (layer, 0, n))],
        out_specs=pl.BlockSpec((tm, tn), lambda n, m: (m, n)),
        out_shape=jax.ShapeDtypeStruct((s, PLAIN_COLS), BF16),
        scratch_shapes=[pltpu.VMEM((d, tn), BF16)],
        compiler_params=_params(2),
        name="proj_plain",
    )(h, w_in)


def _lattice_kernel(a_ref, wq_ref, wk_ref, wv_ref, o_ref, w16_ref, acc_ref, tmp_ref, *, r):
    @pl.when(pl.program_id(0) == 0)
    def _():
        for j, w_ref in enumerate((wq_ref, wk_ref, wv_ref)):
            _cast_rows(w_ref, w16_ref.at[j])

    a = a_ref[...]
    rows = a.shape[0] // r
    slabs = COL // LANES
    for j in range(3):
        acc = jnp.dot(a, w16_ref[j], preferred_element_type=F32)
        for s in range(slabs):
            acc_ref[j * slabs + s] = acc[:, s * LANES:(s + 1) * LANES]
        for s in range(slabs):
            slab = acc_ref.at[j * slabs + s]
            col = j * COL + s * LANES
            if r == LATTICE_STRIDE:
                for c in range(r):
                    o_ref[c, :, col:col + LANES] = slab[pl.ds(c, rows, stride=r), :].astype(o_ref.dtype)
                continue
            assert r == LATTICE_STRIDE ** 2
            tmp = tmp_ref.at[j * slabs + s]
            for c0 in range(LATTICE_STRIDE):
                tmp[c0] = slab[pl.ds(c0, rows * LATTICE_STRIDE, stride=LATTICE_STRIDE), :]
            for c0 in range(LATTICE_STRIDE):
                for c1 in range(LATTICE_STRIDE):
                    o_ref[c0 + LATTICE_STRIDE * c1, :, col:col + LANES] = (
                        tmp[c0, pl.ds(c1, rows, stride=LATTICE_STRIDE), :].astype(o_ref.dtype))


def _norm_proj_kernel(x_ref, g_ref, wq_ref, wk_ref, wv_ref, o_ref, h_ref, w16_ref):
    @pl.when(pl.program_id(0) == 0)
    def _():
        for j, w_ref in enumerate((wq_ref, wk_ref, wv_ref)):
            _cast_rows(w_ref, w16_ref.at[j])

    half = x_ref.shape[0] // 2
    for rows in (slice(0, half), slice(half, 2 * half)):
        h = _rms(x_ref[rows, :], g_ref[...]).astype(h_ref.dtype)
        h_ref[rows, :] = h
        for j in range(3):
            o_ref[0, rows, j * COL:(j + 1) * COL] = jnp.dot(
                h, w16_ref[j], preferred_element_type=F32).astype(o_ref.dtype)


def norm_proj_undilated(x, g, w_in, layer, tm=1024):
    s, d = x.shape

    def w_spec(j):
        return pl.BlockSpec((None, d, COL), lambda m: (layer, 0, QC_BLOCK0 + 3 * j),
                            pipeline_mode=pl.Buffered(1))

    return pl.pallas_call(
        _norm_proj_kernel,
        grid=(s // tm,),
        in_specs=[pl.BlockSpec((tm, d), lambda m: (m, 0)), pl.BlockSpec((1, d), lambda m: (0, 0)),
                  w_spec(0), w_spec(1), w_spec(2)],
        out_specs=[pl.BlockSpec((1, tm, 3 * COL), lambda m: (0, m, 0)),
                   pl.BlockSpec((tm, d), lambda m: (m, 0))],
        out_shape=[jax.ShapeDtypeStruct((1, s, 3 * COL), BF16), jax.ShapeDtypeStruct((s, d), BF16)],
        scratch_shapes=[pltpu.VMEM((3, d, COL), BF16)],
        compiler_params=_params(1),
        name="norm_proj_undilated",
    )(x, g.reshape(1, d), w_in, w_in, w_in)


def proj_lattice(h, w_in, layer, group, r, tm=1024):
    s, d = h.shape
    assert r in (LATTICE_STRIDE, LATTICE_STRIDE ** 2)
    n_slabs = 3 * COL // LANES
    two_pass_rows = tm // LATTICE_STRIDE if r == LATTICE_STRIDE ** 2 else SUBLANES

    def w_spec(j):
        return pl.BlockSpec((None, d, COL), lambda m: (layer, 0, QC_BLOCK0 + group + 3 * j),
                            pipeline_mode=pl.Buffered(1))

    return pl.pallas_call(
        functools.partial(_lattice_kernel, r=r),
        grid=(s // tm,),
        in_specs=[pl.BlockSpec((tm, d), lambda m: (m, 0)), w_spec(0), w_spec(1), w_spec(2)],
        out_specs=pl.BlockSpec((r, tm // r, 3 * COL), lambda m: (0, m, 0)),
        out_shape=jax.ShapeDtypeStruct((r, s // r, 3 * COL), BF16),
        scratch_shapes=[pltpu.VMEM((3, d, COL), BF16),
                        pltpu.VMEM((n_slabs, tm, LANES), F32),
                        pltpu.VMEM((n_slabs, LATTICE_STRIDE, two_pass_rows, LANES), F32)],
        compiler_params=_params(1),
        name=f"proj_lattice_r{r}",
    )(h, w_in, w_in, w_in)


def _merge_kernel(h_ref, ca_ref, ob_ref, oc_ref, wg0_ref, wg1_ref, wg2_ref,
                  wa_ref, wb_ref, wc_ref, o_ref, wg16_ref, wa16_ref, wb16_ref, wc16_ref):
    @pl.when(pl.program_id(1) == 0)
    def _():
        for b, wg_ref in enumerate((wg0_ref, wg1_ref, wg2_ref)):
            _cast_rows(wg_ref, wg16_ref.at[b])
        _cast_rows(wa_ref, wa16_ref)
        _cast_rows(wb_ref, wb16_ref)
        _cast_rows(wc_ref, wc16_ref)

    h = h_ref[...]
    ys = (ca_ref[...], ob_ref[...], oc_ref[...])
    ws = (wa16_ref, wb16_ref, wc16_ref)
    for c0 in range(0, o_ref.shape[1], MXU_COLS):
        cols = slice(c0, c0 + MXU_COLS)
        mixed = None
        for b in range(N_BRANCHES):
            gate = jax.nn.sigmoid(jnp.dot(h, wg16_ref[b, :, cols], preferred_element_type=F32))
            term = gate * jnp.dot(ys[b], ws[b][:, cols], preferred_element_type=F32)
            mixed = term if mixed is None else mixed + term
        o_ref[:, cols] = mixed.astype(o_ref.dtype)


def gated_merge(h, ca, ob, oc, w_in, w_a, w_b, w_c, layer, tm=1024, tn=COL):
    s, d = h.shape
    nj = d // tn
    widths = (ca.shape[1], ob.shape[1], oc.shape[1])

    def gate_spec(b):
        return pl.BlockSpec((None, d, tn), lambda n, m: (layer, 0, GATE_BLOCK0 + b * nj + n),
                            pipeline_mode=pl.Buffered(1))

    def lhs_spec(width):
        return pl.BlockSpec((tm, width), lambda n, m: (m, 0))

    def w_spec(width):
        return pl.BlockSpec((None, width, tn), lambda n, m: (layer, 0, n),
                            pipeline_mode=pl.Buffered(1))

    return pl.pallas_call(
        _merge_kernel,
        grid=(nj, s // tm),
        in_specs=[lhs_spec(d)] + [lhs_spec(w) for w in widths]
        + [gate_spec(0), gate_spec(1), gate_spec(2)] + [w_spec(w) for w in widths],
        out_specs=pl.BlockSpec((tm, tn), lambda n, m: (m, n)),
        out_shape=jax.ShapeDtypeStruct((s, d), BF16),
        scratch_shapes=[pltpu.VMEM((N_BRANCHES, d, tn), BF16)]
        + [pltpu.VMEM((w, tn), BF16) for w in widths],
        compiler_params=_params(2),
        name="gated_merge",
    )(h, ca, ob, oc, w_in, w_in, w_in, w_a, w_b, w_c)


def _out_proj_kernel(a_ref, w_ref, x_ref, g_ref, xo_ref, ho_ref, w16_ref):
    @pl.when(pl.program_id(0) == 0)
    def _():
        _cast_rows(w_ref, w16_ref)

    half = a_ref.shape[0] // 2
    for rows in (slice(0, half), slice(half, 2 * half)):
        x = x_ref[rows, :] + jnp.dot(a_ref[rows, :], w16_ref[...], preferred_element_type=F32)
        xo_ref[rows, :] = x
        ho_ref[rows, :] = _rms(x, g_ref[...]).astype(ho_ref.dtype)


def out_proj_norm(a, w_o, layer, x, g, tm=512):
    s, k = a.shape
    d = x.shape[1]
    row = pl.BlockSpec((tm, d), lambda m: (m, 0))
    return pl.pallas_call(
        _out_proj_kernel,
        grid=(s // tm,),
        in_specs=[pl.BlockSpec((tm, k), lambda m: (m, 0)),
                  pl.BlockSpec((None, k, d), lambda m: (layer, 0, 0), pipeline_mode=pl.Buffered(1)),
                  row,
                  pl.BlockSpec((1, d), lambda m: (0, 0))],
        out_specs=[row, row],
        out_shape=[jax.ShapeDtypeStruct((s, d), F32), jax.ShapeDtypeStruct((s, d), BF16)],
        scratch_shapes=[pltpu.VMEM((k, d), BF16)],
        compiler_params=_params(1),
        name="out_proj_norm",
    )(a, w_o, x, g.reshape(1, d))


def _swiglu_kernel(h_ref, wg_ref, wu_ref, o_ref, w16_ref):
    @pl.when(pl.program_id(1) == 0)
    def _():
        _cast_rows(wg_ref, w16_ref.at[0])
        _cast_rows(wu_ref, w16_ref.at[1])

    h = h_ref[...]
    for c0 in range(0, o_ref.shape[1], MXU_COLS):
        cols = slice(c0, c0 + MXU_COLS)
        g = jnp.dot(h, w16_ref[0, :, cols], preferred_element_type=F32)
        u = jnp.dot(h, w16_ref[1, :, cols], preferred_element_type=F32)
        o_ref[:, cols] = (jax.nn.silu(g) * u).astype(o_ref.dtype)


def swiglu_in(h, w_ffn_in, layer, tm=2048, tn=COL):
    s, d = h.shape
    nj = D_FF // tn
    return pl.pallas_call(
        _swiglu_kernel,
        grid=(nj, s // tm),
        in_specs=[pl.BlockSpec((tm, d), lambda n, m: (m, 0)),
                  pl.BlockSpec((None, d, tn), lambda n, m: (layer, 0, n)),
                  pl.BlockSpec((None, d, tn), lambda n, m: (layer, 0, nj + n))],
        out_specs=pl.BlockSpec((tm, tn), lambda n, m: (m, n)),
        out_shape=jax.ShapeDtypeStruct((s, D_FF), BF16),
        scratch_shapes=[pltpu.VMEM((2, d, tn), BF16)],
        compiler_params=_params(2),
        name="swiglu_in",
    )(h, w_ffn_in, w_ffn_in)


def _mm_residual_kernel(a_ref, w_ref, x_ref, o_ref):
    o_ref[...] = x_ref[...] + jnp.dot(a_ref[...], w_ref[...], preferred_element_type=F32)


def matmul_residual(a, w16, x, tm=1024, tn=COL):
    s, k = a.shape
    d = w16.shape[1]
    return pl.pallas_call(
        _mm_residual_kernel,
        grid=(s // tm, d // tn),
        in_specs=[pl.BlockSpec((tm, k), lambda i, j: (i, 0)),
                  pl.BlockSpec((k, tn), lambda i, j: (0, j)),
                  pl.BlockSpec((tm, tn), lambda i, j: (i, j))],
        out_specs=pl.BlockSpec((tm, tn), lambda i, j: (i, j)),
        out_shape=jax.ShapeDtypeStruct((s, d), F32),
        compiler_params=_params(2),
        name="matmul_residual",
    )(a, w16, x)


def _conv_tile(up_ref, uc_ref, un_ref, w_ref, b_ref, g_ref, beta_ref, o_ref, z_ref, sh_ref,
               *, first, last, chunk=64):
    tm, halo = CONV_TILE, CONV_HALO

    def glu(u):
        u = u.astype(F32)
        return u[:, :CONV_CH] * jax.nn.sigmoid(u[:, CONV_CH:])

    z_ref[0:halo] = jnp.where(first, 0.0, glu(up_ref[...]))
    z_ref[halo:halo + tm] = glu(uc_ref[...])
    z_ref[halo + tm:2 * halo + tm] = jnp.where(last, 0.0, glu(un_ref[...]))
    span = sh_ref.shape[1]
    for j in range(1, SUBLANES):
        sh_ref[j - 1] = z_ref[j:j + span, :]

    for rc in range(tm // chunk):
        acc = jnp.zeros((chunk, CONV_CH), F32)
        for k in range(CONV_WIDTH):
            off = rc * chunk + halo - CONV_HALF + k
            j = off % SUBLANES
            src = z_ref if j == 0 else sh_ref.at[j - 1]
            acc = acc + src[off - j:off - j + chunk, :] * w_ref[k:k + 1, :]
        z = acc + b_ref[...]
        mu = jnp.mean(z, axis=-1, keepdims=True)
        zc = z - mu
        var = jnp.mean(zc * zc, axis=-1, keepdims=True)
        y = zc * lax.rsqrt(var + EPS) * g_ref[...] + beta_ref[...]
        o_ref[rc * chunk:(rc + 1) * chunk, :] = jax.nn.silu(y).astype(o_ref.dtype)


def _conv_kernel(*refs):
    i = pl.program_id(0)
    _conv_tile(*refs, first=i == 0, last=i == pl.num_programs(0) - 1)


def conv_module(plain, conv_w, conv_b, cnorm_g, cnorm_b, layer):
    s = plain.shape[0]
    nh = CONV_TILE // CONV_HALO
    width = 2 * CONV_CH
    vec_spec = pl.BlockSpec((None, 1, CONV_CH), lambda i: (layer, 0, 0))
    row = lambda v: v.reshape(v.shape[0], 1, CONV_CH)
    return pl.pallas_call(
        _conv_kernel,
        grid=(s // CONV_TILE,),
        in_specs=[pl.BlockSpec((CONV_HALO, width), lambda i: (jnp.maximum(i * nh - 1, 0), 0)),
                  pl.BlockSpec((CONV_TILE, width), lambda i: (i, 0)),
                  pl.BlockSpec((CONV_HALO, width),
                               lambda i: (jnp.minimum((i + 1) * nh, s // CONV_HALO - 1), 0)),
                  pl.BlockSpec((None, CONV_WIDTH, CONV_CH), lambda i: (layer, 0, 0)),
                  vec_spec, vec_spec, vec_spec],
        out_specs=pl.BlockSpec((CONV_TILE, CONV_CH), lambda i: (i, 0)),
        out_shape=jax.ShapeDtypeStruct((s, CONV_CH), BF16),
        scratch_shapes=[pltpu.VMEM((CONV_TILE + 2 * CONV_HALO, CONV_CH), F32),
                        pltpu.VMEM((SUBLANES - 1, CONV_TILE + 2 * CONV_HALO - SUBLANES, CONV_CH), F32)],
        compiler_params=_params(1),
        name="conv_module",
    )(plain, plain, plain, conv_w, row(conv_b), row(cnorm_g), row(cnorm_b))


def _band_tables(bq, half, step):
    shape = (bq, bq + 2 * half)
    qi = lax.broadcasted_iota(jnp.int32, shape, 0)
    kj = lax.broadcasted_iota(jnp.int32, shape, 1)
    rel = jnp.abs(kj - half - qi)
    return rel <= half, (step * rel).astype(F32), kj


def _edge_valid(band, kj, bq, half, blk, n_blocks, first, last):
    valid = band
    if first:
        valid = valid & (kj >= jnp.where(blk > 0, 0, half))
    if last:
        valid = valid & (kj < jnp.where(blk < n_blocks - 1, bq + 2 * half, bq + half))
    return valid


def _scores(q, k):
    return lax.dot_general(q, k, (((1,), (1,)), ((), ())), preferred_element_type=F32)


def _softmax_rows(sc, bias2, valid, sink2):
    t = jnp.where(valid, sc * (HEAD_DIM ** -0.5 * LOG2E) + bias2, NEG * LOG2E)
    m2 = jnp.max(t, axis=-1, keepdims=True)
    if sink2 is not None:
        m2 = jnp.maximum(m2, sink2)
    p = jnp.exp2(t - m2)
    denom = jnp.sum(p, axis=-1, keepdims=True)
    if sink2 is not None:
        denom = denom + jnp.exp2(sink2 - m2)
    return p, m2, denom


def _swa_kernel(slopes_ref, sink_ref, q_ref, kp_ref, kc_ref, kn_ref, vp_ref, vc_ref, vn_ref, o_ref,
                *, tq, half, n_blocks, layer):
    i = pl.program_id(0)
    hk = pl.program_id(1)
    bq = half
    nsub = tq // bq
    kcat = jnp.concatenate([kp_ref[...], kc_ref[...], kn_ref[...]], axis=0)
    vcat = jnp.concatenate([vp_ref[...], vc_ref[...], vn_ref[...]], axis=0)
    band, dist, kj = _band_tables(bq, half, 1)
    heads = [hk * B_GROUP + g for g in range(B_GROUP)]
    biases = [(-LOG2E * slopes_ref[hd]) * dist for hd in heads]
    sinks = [LOG2E * sink_ref[layer, hd] for hd in heads]
    for s in range(nsub):
        rows = slice(s * bq, (s + 1) * bq)
        win = slice(s * bq, (s + 1) * bq + 2 * half)
        valid = _edge_valid(band, kj, bq, half, i * nsub + s, n_blocks, s == 0, s == nsub - 1)
        q_all = jnp.concatenate(
            [q_ref[rows, g * HEAD_DIM:(g + 1) * HEAD_DIM] for g in range(B_GROUP)], axis=0)
        sc_all = _scores(q_all, kcat[win])
        ps, denoms = [], []
        for g in range(B_GROUP):
            p, _, denom = _softmax_rows(sc_all[g * bq:(g + 1) * bq], biases[g], valid, sinks[g])
            ps.append(p.astype(BF16))
            denoms.append(denom)
        o_all = jnp.dot(jnp.concatenate(ps, axis=0), vcat[win], preferred_element_type=F32)
        for g in range(B_GROUP):
            o = o_all[g * bq:(g + 1) * bq] / denoms[g]
            o_ref[rows, g * HEAD_DIM:(g + 1) * HEAD_DIM] = o.astype(o_ref.dtype)


def windowed_gqa(plain, slopes, sink, layer, tq=1024):
    s = plain.shape[0]
    half = B_HALF_WINDOW
    nh = tq // half
    qw = B_GROUP * HEAD_DIM
    q_blk0 = 2 * CONV_CH // qw
    k_blk0 = (2 * CONV_CH + B_Q_HEADS * HEAD_DIM) // HEAD_DIM
    v_blk0 = k_blk0 + B_KV_HEADS

    def halo_specs(blk0):
        return [pl.BlockSpec((half, HEAD_DIM), lambda i, h: (jnp.maximum(i * nh - 1, 0), blk0 + h)),
                pl.BlockSpec((tq, HEAD_DIM), lambda i, h: (i, blk0 + h)),
                pl.BlockSpec((half, HEAD_DIM),
                             lambda i, h: (jnp.minimum((i + 1) * nh, s // half - 1), blk0 + h))]

    smem = pl.BlockSpec(memory_space=pltpu.SMEM)
    return pl.pallas_call(
        functools.partial(_swa_kernel, tq=tq, half=half, n_blocks=s // half, layer=layer),
        grid=(s // tq, B_KV_HEADS),
        in_specs=[smem, smem, pl.BlockSpec((tq, qw), lambda i, h: (i, q_blk0 + h))]
        + halo_specs(k_blk0) + halo_specs(v_blk0),
        out_specs=pl.BlockSpec((tq, qw), lambda i, h: (i, h)),
        out_shape=jax.ShapeDtypeStruct((s, B_Q_HEADS * HEAD_DIM), BF16),
        compiler_params=_params(2),
        name="windowed_gqa",
    )(slopes, sink, plain, plain, plain, plain, plain, plain, plain)


def _dilated_kernel(slopes_ref, *refs, tokens, bq, half, seq_len):
    o_ref, u_scr, m_scr, d_scr = refs[-4:]
    i = pl.program_id(0)
    h = pl.program_id(1)
    for gi, (_, r) in enumerate(C_PATTERNS):
        q_ref, kp_ref, kc_ref, kn_ref, vp_ref, vc_ref, vn_ref = refs[7 * gi:7 * gi + 7]
        rows = tokens // r
        nsub = rows // bq
        n_blocks = seq_len // r // bq
        band, dist, kj = _band_tables(bq, half, r)
        bias = (-LOG2E * slopes_ref[B_Q_HEADS + gi * C_HEADS_PER_GROUP + h]) * dist
        for c in range(r):
            kcat = jnp.concatenate([kp_ref[c], kc_ref[c], kn_ref[c]], axis=0)
            vcat = jnp.concatenate([vp_ref[c], vc_ref[c], vn_ref[c]], axis=0)
            for s in range(nsub):
                win = slice(s * bq, (s + 1) * bq + 2 * half)
                valid = _edge_valid(band, kj, bq, half, i * nsub + s, n_blocks, s == 0, s == nsub - 1)
                p, m2, denom = _softmax_rows(_scores(q_ref[c, s * bq:(s + 1) * bq, :], kcat[win]),
                                             bias, valid, None)
                dst = pl.ds(s * bq * r + c, bq, stride=r) if r > 1 else pl.ds(s * bq, bq)
                u_scr[gi, dst, :] = jnp.dot(p.astype(BF16), vcat[win], preferred_element_type=F32)
                m_scr[gi, dst, :] = jnp.broadcast_to(m2, (bq, HEAD_DIM))
                d_scr[gi, dst, :] = jnp.broadcast_to(denom, (bq, HEAD_DIM))
    m_max = jnp.maximum(jnp.maximum(m_scr[0], m_scr[1]), m_scr[2])
    num = den = None
    for gi in range(len(C_PATTERNS)):
        w = jnp.exp2(m_scr[gi] - m_max)
        num = w * u_scr[gi] if num is None else num + w * u_scr[gi]
        den = w * d_scr[gi] if den is None else den + w * d_scr[gi]
    o_ref[...] = (num / den).astype(o_ref.dtype)


def dilated_attention(lattices, slopes, seq_len, tokens=C_TOKENS, bq=C_BLOCK_Q, half=C_HALF):
    nq = C_HEADS_PER_GROUP
    in_specs = [pl.BlockSpec(memory_space=pltpu.SMEM)]
    args = [slopes]
    for (_, r), lat in zip(C_PATTERNS, lattices):
        rows = tokens // r
        nh = rows // half
        last = seq_len // r // half - 1

        def specs(blk0, r=r, rows=rows, nh=nh, last=last):
            return [pl.BlockSpec((r, half, HEAD_DIM), lambda i, h: (0, jnp.maximum(i * nh - 1, 0), blk0 + h)),
                    pl.BlockSpec((r, rows, HEAD_DIM), lambda i, h: (0, i, blk0 + h)),
                    pl.BlockSpec((r, half, HEAD_DIM), lambda i, h: (0, jnp.minimum((i + 1) * nh, last), blk0 + h))]

        in_specs += [pl.BlockSpec((r, rows, HEAD_DIM), lambda i, h: (0, i, h))] + specs(nq) + specs(2 * nq)
        args += [lat] * 7
    return pl.pallas_call(
        functools.partial(_dilated_kernel, tokens=tokens, bq=bq, half=half, seq_len=seq_len),
        grid=(seq_len // tokens, nq),
        in_specs=in_specs,
        out_specs=pl.BlockSpec((tokens, HEAD_DIM), lambda i, h: (i, h)),
        out_shape=jax.ShapeDtypeStruct((seq_len, nq * HEAD_DIM), BF16),
        scratch_shapes=[pltpu.VMEM((len(C_PATTERNS), tokens, HEAD_DIM), F32)] * 3,
        compiler_params=_params(2),
        name="dilated_attention",
    )(*args)


def _alibi_slopes():
    h = jnp.arange(1, N_ATTN_HEADS + 1, dtype=F32)
    return jnp.exp2(-ALIBI_MAX_EXP * h / N_ATTN_HEADS)


def kernel(x, ln1_g, w_in, conv_w, conv_b, cnorm_g, cnorm_b, w_a, sink, w_b, w_c, w_o, ln2_g,
           w_ffn_in, w_ffn_out, lnf_g):
    b, s, d = x.shape
    depth = ln1_g.shape[0]
    slopes = _alibi_slopes()
    outs = []
    for xb in jnp.split(x, b, axis=0):
        xb = xb.reshape(s, d)
        for l in range(depth):
            lat0, h = norm_proj_undilated(xb, ln1_g[l], w_in, l)
            plain = proj_plain(h, w_in, l)
            lattices = [lat0] + [proj_lattice(h, w_in, l, gi, r)
                                 for gi, (_, r) in enumerate(C_PATTERNS) if gi > 0]
            ca = conv_module(plain, conv_w, conv_b, cnorm_g, cnorm_b, l)
            ob = windowed_gqa(plain, slopes, sink, l)
            oc = dilated_attention(lattices, slopes, s)
            mixed = gated_merge(h, ca, ob, oc, w_in, w_a, w_b, w_c, l)
            xb, h2 = out_proj_norm(mixed, w_o, l, xb, ln2_g[l])
            act = swiglu_in(h2, w_ffn_in, l)
            xb = matmul_residual(act, w_ffn_out[l].astype(BF16), xb)
        outs.append(rmsnorm(xb, lnf_g, x.dtype).reshape(1, s, d))
    return outs[0] if b == 1 else jnp.concatenate(outs, axis=0)
```

```python
import functools

import jax
import jax.numpy as jnp
from jax import lax
from jax.experimental import pallas as pl
from jax.experimental.pallas import tpu as pltpu

F32 = jnp.float32
BF16 = jnp.bfloat16

D_MODEL = 2048
HEAD_DIM = 128
CONV_CH = 512
CONV_WIDTH = 31
CONV_HALF = CONV_WIDTH // 2
B_Q_HEADS = 8
B_KV_HEADS = 2
B_GROUP = B_Q_HEADS // B_KV_HEADS
B_HALF_WINDOW = 128
C_PATTERNS = ((128, 1), (512, 4), (2048, 16))
C_HEADS_PER_GROUP = 4
C_HALF = 64
assert C_PATTERNS[0][1] == 1 and all(w // (2 * r) == C_HALF for w, r in C_PATTERNS)
N_BRANCHES = 3
N_ATTN_HEADS = B_Q_HEADS + len(C_PATTERNS) * C_HEADS_PER_GROUP
ALIBI_MAX_EXP = 8.0
D_FF = -(-8 * D_MODEL // (3 * 256)) * 256
EPS = 1e-6
NEG = -1e30
LOG2E = 1.4426950408889634

LANES = 128
MXU_COLS = 256
SUBLANES = 8
LATTICE_STRIDE = 4
COL = 512
PLAIN_COLS = 5 * COL
QC_BLOCK0 = 5
GATE_BLOCK0 = 14
CONV_TILE = 512
CONV_HALO = 16
C_TOKENS = 2048
C_BLOCK_Q = 128

VMEM_LIMIT = 56 * 1024 * 1024


def _params(n_axes, flags=None):
    return pltpu.CompilerParams(
        dimension_semantics=("arbitrary",) * n_axes, vmem_limit_bytes=VMEM_LIMIT, flags=flags)


def _cast_rows(src_ref, dst_ref, chunk=256):
    rows = src_ref.shape[0]
    chunk = min(chunk, rows)

    def body(c, carry):
        sl = pl.ds(pl.multiple_of(c * chunk, chunk), chunk)
        dst_ref[sl, :] = src_ref[sl, :].astype(dst_ref.dtype)
        return carry

    lax.fori_loop(0, rows // chunk, body, 0)


def _rms(x, g):
    ms = jnp.mean(x * x, axis=-1, keepdims=True)
    return x * lax.rsqrt(ms + EPS) * g


def _ws_mm_kernel(a_ref, w_ref, o_ref, w16_ref):
    @pl.when(pl.program_id(1) == 0)
    def _():
        _cast_rows(w_ref, w16_ref)

    o_ref[...] = jnp.dot(a_ref[...], w16_ref[...],
                         preferred_element_type=F32).astype(o_ref.dtype)


def proj_plain(h, w_in, layer, tm=1024, tn=PLAIN_COLS // 2):
    s, d = h.shape
    return pl.pallas_call(
        _ws_mm_kernel,
        grid=(PLAIN_COLS // tn, s // tm),
        in_specs=[pl.BlockSpec((tm, d), lambda n, m: (m, 0)),
                  pl.BlockSpec((None, d, tn), lambda n, m: (layer, 0, n))],
        out_specs=pl.BlockSpec((tm, tn), lambda n, m: (m, n)),
        out_shape=jax.ShapeDtypeStruct((s, PLAIN_COLS), BF16),
        scratch_shapes=[pltpu.VMEM((d, tn), BF16)],
        compiler_params=_params(2),
        name="proj_plain",
    )(h, w_in)


def _lattice_kernel(a_ref, wq_ref, wk_ref, wv_ref, o_ref, w16_ref, acc_ref, tmp_ref, *, r):
    @pl.when(pl.program_id(0) == 0)
    def _():
        for j, w_ref in enumerate((wq_ref, wk_ref, wv_ref)):
            _cast_rows(w_ref, w16_ref.at[j])

    a = a_ref[...]
    rows = a.shape[0] // r
    slabs = COL // LANES
    for j in range(3):
        acc = jnp.dot(a, w16_ref[j], preferred_element_type=F32)
        for s in range(slabs):
            acc_ref[j * slabs + s] = acc[:, s * LANES:(s + 1) * LANES]
        for s in range(slabs):
            slab = acc_ref.at[j * slabs + s]
            col = j * COL + s * LANES
            if r == LATTICE_STRIDE:
                for c in range(r):
                    o_ref[c, :, col:col + LANES] = slab[pl.ds(c, rows, stride=r), :].astype(o_ref.dtype)
                continue
            assert r == LATTICE_STRIDE ** 2
            tmp = tmp_ref.at[j * slabs + s]
            for c0 in range(LATTICE_STRIDE):
                tmp[c0] = slab[pl.ds(c0, rows * LATTICE_STRIDE, stride=LATTICE_STRIDE), :]
            for c0 in range(LATTICE_STRIDE):
                for c1 in range(LATTICE_STRIDE):
                    o_ref[c0 + LATTICE_STRIDE * c1, :, col:col + LANES] = (
                        tmp[c0, pl.ds(c1, rows, stride=LATTICE_STRIDE), :].astype(o_ref.dtype))


def _norm_proj_kernel(x_ref, g_ref, wq_ref, wk_ref, wv_ref, o_ref, h_ref, w16_ref):
    @pl.when(pl.program_id(0) == 0)
    def _():
        for j, w_ref in enumerate((wq_ref, wk_ref, wv_ref)):
            _cast_rows(w_ref, w16_ref.at[j])

    half = x_ref.shape[0] // 2
    for rows in (slice(0, half), slice(half, 2 * half)):
        h = _rms(x_ref[rows, :], g_ref[...]).astype(h_ref.dtype)
        h_ref[rows, :] = h
        for j in range(3):
            o_ref[0, rows, j * COL:(j + 1) * COL] = jnp.dot(
                h, w16_ref[j], preferred_element_type=F32).astype(o_ref.dtype)


def norm_proj_undilated(x, g, w_in, layer, tm=1024):
    s, d = x.shape

    def w_spec(j):
        return pl.BlockSpec((None, d, COL), lambda m: (layer, 0, QC_BLOCK0 + 3 * j),
                            pipeline_mode=pl.Buffered(1))

    return pl.pallas_call(
        _norm_proj_kernel,
        grid=(s // tm,),
        in_specs=[pl.BlockSpec((tm, d), lambda m: (m, 0)), pl.BlockSpec((1, d), lambda m: (0, 0)),
                  w_spec(0), w_spec(1), w_spec(2)],
        out_specs=[pl.BlockSpec((1, tm, 3 * COL), lambda m: (0, m, 0)),
                   pl.BlockSpec((tm, d), lambda m: (m, 0))],
        out_shape=[jax.ShapeDtypeStruct((1, s, 3 * COL), BF16), jax.ShapeDtypeStruct((s, d), BF16)],
        scratch_shapes=[pltpu.VMEM((3, d, COL), BF16)],
        compiler_params=_params(1),
        name="norm_proj_undilated",
    )(x, g.reshape(1, d), w_in, w_in, w_in)


def proj_lattice(h, w_in, layer, group, r, tm=1024):
    s, d = h.shape
    assert r in (LATTICE_STRIDE, LATTICE_STRIDE ** 2)
    n_slabs = 3 * COL // LANES
    two_pass_rows = tm // LATTICE_STRIDE if r == LATTICE_STRIDE ** 2 else SUBLANES

    def w_spec(j):
        return pl.BlockSpec((None, d, COL), lambda m: (layer, 0, QC_BLOCK0 + group + 3 * j),
                            pipeline_mode=pl.Buffered(1))

    return pl.pallas_call(
        functools.partial(_lattice_kernel, r=r),
        grid=(s // tm,),
        in_specs=[pl.BlockSpec((tm, d), lambda m: (m, 0)), w_spec(0), w_spec(1), w_spec(2)],
        out_specs=pl.BlockSpec((r, tm // r, 3 * COL), lambda m: (0, m, 0)),
        out_shape=jax.ShapeDtypeStruct((r, s // r, 3 * COL), BF16),
        scratch_shapes=[pltpu.VMEM((3, d, COL), BF16),
                        pltpu.VMEM((n_slabs, tm, LANES), F32),
                        pltpu.VMEM((n_slabs, LATTICE_STRIDE, two_pass_rows, LANES), F32)],
        compiler_params=_params(1),
        name=f"proj_lattice_r{r}",
    )(h, w_in, w_in, w_in)


def _merge_kernel(h_ref, ca_ref, ob_ref, oc_ref, wg0_ref, wg1_ref, wg2_ref,
                  wa_ref, wb_ref, wc_ref, o_ref, wg16_ref, wa16_ref, wb16_ref, wc16_ref):
    @pl.when(pl.program_id(1) == 0)
    def _():
        for b, wg_ref in enumerate((wg0_ref, wg1_ref, wg2_ref)):
            _cast_rows(wg_ref, wg16_ref.at[b])
        _cast_rows(wa_ref, wa16_ref)
        _cast_rows(wb_ref, wb16_ref)
        _cast_rows(wc_ref, wc16_ref)

    h = h_ref[...]
    ys = (ca_ref[...], ob_ref[...], oc_ref[...])
    ws = (wa16_ref, wb16_ref, wc16_ref)
    for c0 in range(0, o_ref.shape[1], MXU_COLS):
        cols = slice(c0, c0 + MXU_COLS)
        mixed = None
        for b in range(N_BRANCHES):
            gate = jax.nn.sigmoid(jnp.dot(h, wg16_ref[b, :, cols], preferred_element_type=F32))
            term = gate * jnp.dot(ys[b], ws[b][:, cols], preferred_element_type=F32)
            mixed = term if mixed is None else mixed + term
        o_ref[:, cols] = mixed.astype(o_ref.dtype)


def gated_merge(h, ca, ob, oc, w_in, w_a, w_b, w_c, layer, tm=1024, tn=COL):
    s, d = h.shape
    nj = d // tn
    widths = (ca.shape[1], ob.shape[1], oc.shape[1])

    def gate_spec(b):
        return pl.BlockSpec((None, d, tn), lambda n, m: (layer, 0, GATE_BLOCK0 + b * nj + n),
                            pipeline_mode=pl.Buffered(1))

    def lhs_spec(width):
        return pl.BlockSpec((tm, width), lambda n, m: (m, 0))

    def w_spec(width):
        return pl.BlockSpec((None, width, tn), lambda n, m: (layer, 0, n),
                            pipeline_mode=pl.Buffered(1))

    return pl.pallas_call(
        _merge_kernel,
        grid=(nj, s // tm),
        in_specs=[lhs_spec(d)] + [lhs_spec(w) for w in widths]
        + [gate_spec(0), gate_spec(1), gate_spec(2)] + [w_spec(w) for w in widths],
        out_specs=pl.BlockSpec((tm, tn), lambda n, m: (m, n)),
        out_shape=jax.ShapeDtypeStruct((s, d), BF16),
        scratch_shapes=[pltpu.VMEM((N_BRANCHES, d, tn), BF16)]
        + [pltpu.VMEM((w, tn), BF16) for w in widths],
        compiler_params=_params(2),
        name="gated_merge",
    )(h, ca, ob, oc, w_in, w_in, w_in, w_a, w_b, w_c)


def _out_proj_kernel(a_ref, w_ref, x_ref, g_ref, xo_ref, ho_ref, w16_ref):
    @pl.when(pl.program_id(0) == 0)
    def _():
        _cast_rows(w_ref, w16_ref)

    half = a_ref.shape[0] // 2
    for rows in (slice(0, half), slice(half, 2 * half)):
        x = x_ref[rows, :] + jnp.dot(a_ref[rows, :], w16_ref[...], preferred_element_type=F32)
        xo_ref[rows, :] = x
        ho_ref[rows, :] = _rms(x, g_ref[...]).astype(ho_ref.dtype)


def out_proj_norm(a, w_o, layer, x, g, tm=512):
    s, k = a.shape
    d = x.shape[1]
    row = pl.BlockSpec((tm, d), lambda m: (m, 0))
    return pl.pallas_call(
        _out_proj_kernel,
        grid=(s // tm,),
        in_specs=[pl.BlockSpec((tm, k), lambda m: (m, 0)),
                  pl.BlockSpec((None, k, d), lambda m: (layer, 0, 0), pipeline_mode=pl.Buffered(1)),
                  row,
                  pl.BlockSpec((1, d), lambda m: (0, 0))],
        out_specs=[row, row],
        out_shape=[jax.ShapeDtypeStruct((s, d), F32), jax.ShapeDtypeStruct((s, d), BF16)],
        scratch_shapes=[pltpu.VMEM((k, d), BF16)],
        compiler_params=_params(1),
        name="out_proj_norm",
    )(a, w_o, x, g.reshape(1, d))


def _swiglu_kernel(h_ref, wg_ref, wu_ref, o_ref, w16_ref):
    @pl.when(pl.program_id(1) == 0)
    def _():
        _cast_rows(wg_ref, w16_ref.at[0])
        _cast_rows(wu_ref, w16_ref.at[1])

    h = h_ref[...]
    for c0 in range(0, o_ref.shape[1], MXU_COLS):
        cols = slice(c0, c0 + MXU_COLS)
        g = jnp.dot(h, w16_ref[0, :, cols], preferred_element_type=F32)
        u = jnp.dot(h, w16_ref[1, :, cols], preferred_element_type=F32)
        o_ref[:, cols] = (jax.nn.silu(g) * u).astype(o_ref.dtype)


def swiglu_in(h, w_ffn_in, layer, tm=2048, tn=COL):
    s, d = h.shape
    nj = D_FF // tn
    return pl.pallas_call(
        _swiglu_kernel,
        grid=(nj, s // tm),
        in_specs=[pl.BlockSpec((tm, d), lambda n, m: (m, 0)),
                  pl.BlockSpec((None, d, tn), lambda n, m: (layer, 0, n)),
                  pl.BlockSpec((None, d, tn), lambda n, m: (layer, 0, nj + n))],
        out_specs=pl.BlockSpec((tm, tn), lambda n, m: (m, n)),
        out_shape=jax.ShapeDtypeStruct((s, D_FF), BF16),
        scratch_shapes=[pltpu.VMEM((2, d, tn), BF16)],
        compiler_params=_params(2),
        name="swiglu_in",
    )(h, w_ffn_in, w_ffn_in)


def _mm_residual_kernel(a_ref, w_ref, x_ref, o_ref):
    o_ref[...] = x_ref[...] + jnp.dot(a_ref[...], w_ref[...], preferred_element_type=F32)


def matmul_residual(a, w16, x, tm=1024, tn=COL):
    s, k = a.shape
    d = w16.shape[1]
    return pl.pallas_call(
        _mm_residual_kernel,
        grid=(s // tm, d // tn),
        in_specs=[pl.BlockSpec((tm, k), lambda i, j: (i, 0)),
                  pl.BlockSpec((k, tn), lambda i, j: (0, j)),
                  pl.BlockSpec((tm, tn), lambda i, j: (i, j))],
        out_specs=pl.BlockSpec((tm, tn), lambda i, j: (i, j)),
        out_shape=jax.ShapeDtypeStruct((s, d), F32),
        compiler_params=_params(2),
        name="matmul_residual",
    )(a, w16, x)


def _mm_residual_norm_kernel(a_ref, w_ref, x_ref, g_ref, o_ref, ss_ref):
    n = pl.program_id(1)
    tn = x_ref.shape[1]
    x = x_ref[...] + jnp.dot(a_ref[...], w_ref[...], preferred_element_type=F32)
    ss = jnp.sum(x * x, axis=-1, keepdims=True)

    @pl.when(n == 0)
    def _():
        ss_ref[...] = jnp.broadcast_to(ss, ss_ref.shape)

    @pl.when(n > 0)
    def _():
        ss_ref[...] += ss

    for j in range(o_ref.shape[1] // tn):
        @pl.when(n == j)
        def _():
            o_ref[:, j * tn:(j + 1) * tn] = x

    @pl.when(n == pl.num_programs(1) - 1)
    def _():
        ms = ss_ref[:, 0:1] * (1.0 / o_ref.shape[1])
        o_ref[...] = o_ref[...] * lax.rsqrt(ms + EPS) * g_ref[...]


def matmul_residual_norm(a, w16, x, g, tm=1024, tn=COL):
    s, k = a.shape
    d = w16.shape[1]
    return pl.pallas_call(
        _mm_residual_norm_kernel,
        grid=(s // tm, d // tn),
        in_specs=[pl.BlockSpec((tm, k), lambda i, j: (i, 0)),
                  pl.BlockSpec((k, tn), lambda i, j: (0, j)),
                  pl.BlockSpec((tm, tn), lambda i, j: (i, j)),
                  pl.BlockSpec((1, d), lambda i, j: (0, 0))],
        out_specs=pl.BlockSpec((tm, d), lambda i, j: (i, 0)),
        out_shape=jax.ShapeDtypeStruct((s, d), F32),
        scratch_shapes=[pltpu.VMEM((tm, LANES), F32)],
        compiler_params=_params(2),
        name="matmul_residual_norm",
    )(a, w16, x, g.reshape(1, d))


def _conv_tile(up_ref, uc_ref, un_ref, w_ref, b_ref, g_ref, beta_ref, o_ref, z_ref, sh_ref,
               *, first, last, chunk=64):
    tm, halo = CONV_TILE, CONV_HALO

    def glu(u):
        u = u.astype(F32)
        return u[:, :CONV_CH] * jax.nn.sigmoid(u[:, CONV_CH:])

    z_ref[0:halo] = jnp.where(first, 0.0, glu(up_ref[...]))
    z_ref[halo:halo + tm] = glu(uc_ref[...])
    z_ref[halo + tm:2 * halo + tm] = jnp.where(last, 0.0, glu(un_ref[...]))
    span = sh_ref.shape[1]
    for j in range(1, SUBLANES):
        sh_ref[j - 1] = z_ref[j:j + span, :]

    for rc in range(tm // chunk):
        acc = jnp.zeros((chunk, CONV_CH), F32)
        for k in range(CONV_WIDTH):
            off = rc * chunk + halo - CONV_HALF + k
            j = off % SUBLANES
            src = z_ref if j == 0 else sh_ref.at[j - 1]
            acc = acc + src[off - j:off - j + chunk, :] * w_ref[k:k + 1, :]
        z = acc + b_ref[...]
        mu = jnp.mean(z, axis=-1, keepdims=True)
        zc = z - mu
        var = jnp.mean(zc * zc, axis=-1, keepdims=True)
        y = zc * lax.rsqrt(var + EPS) * g_ref[...] + beta_ref[...]
        o_ref[rc * chunk:(rc + 1) * chunk, :] = jax.nn.silu(y).astype(o_ref.dtype)


def _conv_kernel(*refs):
    i = pl.program_id(0)
    _conv_tile(*refs, first=i == 0, last=i == pl.num_programs(0) - 1)


def conv_module(plain, conv_w, conv_b, cnorm_g, cnorm_b, layer):
    s = plain.shape[0]
    nh = CONV_TILE // CONV_HALO
    width = 2 * CONV_CH
    vec_spec = pl.BlockSpec((None, 1, CONV_CH), lambda i: (layer, 0, 0))
    row = lambda v: v.reshape(v.shape[0], 1, CONV_CH)
    return pl.pallas_call(
        _conv_kernel,
        grid=(s // CONV_TILE,),
        in_specs=[pl.BlockSpec((CONV_HALO, width), lambda i: (jnp.maximum(i * nh - 1, 0), 0)),
                  pl.BlockSpec((CONV_TILE, width), lambda i: (i, 0)),
                  pl.BlockSpec((CONV_HALO, width),
                               lambda i: (jnp.minimum((i + 1) * nh, s // CONV_HALO - 1), 0)),
                  pl.BlockSpec((None, CONV_WIDTH, CONV_CH), lambda i: (layer, 0, 0)),
                  vec_spec, vec_spec, vec_spec],
        out_specs=pl.BlockSpec((CONV_TILE, CONV_CH), lambda i: (i, 0)),
        out_shape=jax.ShapeDtypeStruct((s, CONV_CH), BF16),
        scratch_shapes=[pltpu.VMEM((CONV_TILE + 2 * CONV_HALO, CONV_CH), F32),
                        pltpu.VMEM((SUBLANES - 1, CONV_TILE + 2 * CONV_HALO - SUBLANES, CONV_CH), F32)],
        compiler_params=_params(1),
        name="conv_module",
    )(plain, plain, plain, conv_w, row(conv_b), row(cnorm_g), row(cnorm_b))


def _band_tables(bq, half, step):
    shape = (bq, bq + 2 * half)
    qi = lax.broadcasted_iota(jnp.int32, shape, 0)
    kj = lax.broadcasted_iota(jnp.int32, shape, 1)
    rel = jnp.abs(kj - half - qi)
    return rel <= half, (step * rel).astype(F32), kj


def _edge_valid(band, kj, bq, half, blk, n_blocks, first, last):
    valid = band
    if first:
        valid = valid & (kj >= jnp.where(blk > 0, 0, half))
    if last:
        valid = valid & (kj < jnp.where(blk < n_blocks - 1, bq + 2 * half, bq + half))
    return valid


def _scores(q, k):
    return lax.dot_general(q, k, (((1,), (1,)), ((), ())), preferred_element_type=F32)


def _softmax_rows(sc, bias2, valid, sink2):
    t = jnp.where(valid, sc * (HEAD_DIM ** -0.5 * LOG2E) + bias2, NEG * LOG2E)
    m2 = jnp.max(t, axis=-1, keepdims=True)
    if sink2 is not None:
        m2 = jnp.maximum(m2, sink2)
    p = jnp.exp2(t - m2)
    denom = jnp.sum(p, axis=-1, keepdims=True)
    if sink2 is not None:
        denom = denom + jnp.exp2(sink2 - m2)
    return p, m2, denom


def _swa_kernel(slopes_ref, sink_ref, q_ref, kp_ref, kc_ref, kn_ref, vp_ref, vc_ref, vn_ref, o_ref,
                *, tq, half, n_blocks, layer):
    i = pl.program_id(0)
    hk = pl.program_id(1)
    bq = half
    nsub = tq // bq
    kcat = jnp.concatenate([kp_ref[...], kc_ref[...], kn_ref[...]], axis=0)
    vcat = jnp.concatenate([vp_ref[...], vc_ref[...], vn_ref[...]], axis=0)
    band, dist, kj = _band_tables(bq, half, 1)
    heads = [hk * B_GROUP + g for g in range(B_GROUP)]
    biases = [(-LOG2E * slopes_ref[hd]) * dist for hd in heads]
    sinks = [LOG2E * sink_ref[layer, hd] for hd in heads]
    for s in range(nsub):
        rows = slice(s * bq, (s + 1) * bq)
        win = slice(s * bq, (s + 1) * bq + 2 * half)
        valid = _edge_valid(band, kj, bq, half, i * nsub + s, n_blocks, s == 0, s == nsub - 1)
        q_all = jnp.concatenate(
            [q_ref[rows, g * HEAD_DIM:(g + 1) * HEAD_DIM] for g in range(B_GROUP)], axis=0)
        sc_all = _scores(q_all, kcat[win])
        ps, denoms = [], []
        for g in range(B_GROUP):
            p, _, denom = _softmax_rows(sc_all[g * bq:(g + 1) * bq], biases[g], valid, sinks[g])
            ps.append(p.astype(BF16))
            denoms.append(denom)
        o_all = jnp.dot(jnp.concatenate(ps, axis=0), vcat[win], preferred_element_type=F32)
        for g in range(B_GROUP):
            o = o_all[g * bq:(g + 1) * bq] / denoms[g]
            o_ref[rows, g * HEAD_DIM:(g + 1) * HEAD_DIM] = o.astype(o_ref.dtype)


def windowed_gqa(plain, slopes, sink, layer, tq=1024):
    s = plain.shape[0]
    half = B_HALF_WINDOW
    nh = tq // half
    qw = B_GROUP * HEAD_DIM
    q_blk0 = 2 * CONV_CH // qw
    k_blk0 = (2 * CONV_CH + B_Q_HEADS * HEAD_DIM) // HEAD_DIM
    v_blk0 = k_blk0 + B_KV_HEADS

    def halo_specs(blk0):
        return [pl.BlockSpec((half, HEAD_DIM), lambda i, h: (jnp.maximum(i * nh - 1, 0), blk0 + h)),
                pl.BlockSpec((tq, HEAD_DIM), lambda i, h: (i, blk0 + h)),
                pl.BlockSpec((half, HEAD_DIM),
                             lambda i, h: (jnp.minimum((i + 1) * nh, s // half - 1), blk0 + h))]

    smem = pl.BlockSpec(memory_space=pltpu.SMEM)
    return pl.pallas_call(
        functools.partial(_swa_kernel, tq=tq, half=half, n_blocks=s // half, layer=layer),
        grid=(s // tq, B_KV_HEADS),
        in_specs=[smem, smem, pl.BlockSpec((tq, qw), lambda i, h: (i, q_blk0 + h))]
        + halo_specs(k_blk0) + halo_specs(v_blk0),
        out_specs=pl.BlockSpec((tq, qw), lambda i, h: (i, h)),
        out_shape=jax.ShapeDtypeStruct((s, B_Q_HEADS * HEAD_DIM), BF16),
        compiler_params=_params(2),
        name="windowed_gqa",
    )(slopes, sink, plain, plain, plain, plain, plain, plain, plain)


def _dilated_kernel(slopes_ref, *refs, tokens, bq, half, seq_len):
    o_ref, u_scr, m_scr, d_scr = refs[-4:]
    i = pl.program_id(0)
    h = pl.program_id(1)
    for gi, (_, r) in enumerate(C_PATTERNS):
        q_ref, kp_ref, kc_ref, kn_ref, vp_ref, vc_ref, vn_ref = refs[7 * gi:7 * gi + 7]
        rows = tokens // r
        nsub = rows // bq
        n_blocks = seq_len // r // bq
        band, dist, kj = _band_tables(bq, half, r)
        bias = (-LOG2E * slopes_ref[B_Q_HEADS + gi * C_HEADS_PER_GROUP + h]) * dist
        for c in range(r):
            kcat = jnp.concatenate([kp_ref[c], kc_ref[c], kn_ref[c]], axis=0)
            vcat = jnp.concatenate([vp_ref[c], vc_ref[c], vn_ref[c]], axis=0)
            for s in range(nsub):
                win = slice(s * bq, (s + 1) * bq + 2 * half)
                valid = _edge_valid(band, kj, bq, half, i * nsub + s, n_blocks, s == 0, s == nsub - 1)
                p, m2, denom = _softmax_rows(_scores(q_ref[c, s * bq:(s + 1) * bq, :], kcat[win]),
                                             bias, valid, None)
                dst = pl.ds(s * bq * r + c, bq, stride=r) if r > 1 else pl.ds(s * bq, bq)
                u_scr[gi, dst, :] = jnp.dot(p.astype(BF16), vcat[win], preferred_element_type=F32)
                m_scr[gi, dst, :] = jnp.broadcast_to(m2, (bq, HEAD_DIM))
                d_scr[gi, dst, :] = jnp.broadcast_to(denom, (bq, HEAD_DIM))
    m_max = jnp.maximum(jnp.maximum(m_scr[0], m_scr[1]), m_scr[2])
    num = den = None
    for gi in range(len(C_PATTERNS)):
        w = jnp.exp2(m_scr[gi] - m_max)
        num = w * u_scr[gi] if num is None else num + w * u_scr[gi]
        den = w * d_scr[gi] if den is None else den + w * d_scr[gi]
    o_ref[...] = (num / den).astype(o_ref.dtype)


def dilated_attention(lattices, slopes, seq_len, tokens=C_TOKENS, bq=C_BLOCK_Q, half=C_HALF):
    nq = C_HEADS_PER_GROUP
    in_specs = [pl.BlockSpec(memory_space=pltpu.SMEM)]
    args = [slopes]
    for (_, r), lat in zip(C_PATTERNS, lattices):
        rows = tokens // r
        nh = rows // half
        last = seq_len // r // half - 1

        def specs(blk0, r=r, rows=rows, nh=nh, last=last):
            return [pl.BlockSpec((r, half, HEAD_DIM), lambda i, h: (0, jnp.maximum(i * nh - 1, 0), blk0 + h)),
                    pl.BlockSpec((r, rows, HEAD_DIM), lambda i, h: (0, i, blk0 + h)),
                    pl.BlockSpec((r, half, HEAD_DIM), lambda i, h: (0, jnp.minimum((i + 1) * nh, last), blk0 + h))]

        in_specs += [pl.BlockSpec((r, rows, HEAD_DIM), lambda i, h: (0, i, h))] + specs(nq) + specs(2 * nq)
        args += [lat] * 7
    return pl.pallas_call(
        functools.partial(_dilated_kernel, tokens=tokens, bq=bq, half=half, seq_len=seq_len),
        grid=(seq_len // tokens, nq),
        in_specs=in_specs,
        out_specs=pl.BlockSpec((tokens, HEAD_DIM), lambda i, h: (i, h)),
        out_shape=jax.ShapeDtypeStruct((seq_len, nq * HEAD_DIM), BF16),
        scratch_shapes=[pltpu.VMEM((len(C_PATTERNS), tokens, HEAD_DIM), F32)] * 3,
        compiler_params=_params(2),
        name="dilated_attention",
    )(*args)


def _alibi_slopes():
    h = jnp.arange(1, N_ATTN_HEADS + 1, dtype=F32)
    return jnp.exp2(-ALIBI_MAX_EXP * h / N_ATTN_HEADS)


def kernel(x, ln1_g, w_in, conv_w, conv_b, cnorm_g, cnorm_b, w_a, sink, w_b, w_c, w_o, ln2_g,
           w_ffn_in, w_ffn_out, lnf_g):
    b, s, d = x.shape
    depth = ln1_g.shape[0]
    assert depth >= 1
    slopes = _alibi_slopes()
    outs = []
    for xb in jnp.split(x, b, axis=0):
        xb = xb.reshape(s, d)
        for l in range(depth):
            lat0, h = norm_proj_undilated(xb, ln1_g[l], w_in, l)
            plain = proj_plain(h, w_in, l)
            lattices = [lat0] + [proj_lattice(h, w_in, l, gi, r)
                                 for gi, (_, r) in enumerate(C_PATTERNS) if gi > 0]
            ca = conv_module(plain, conv_w, conv_b, cnorm_g, cnorm_b, l)
            ob = windowed_gqa(plain, slopes, sink, l)
            oc = dilated_attention(lattices, slopes, s)
            mixed = gated_merge(h, ca, ob, oc, w_in, w_a, w_b, w_c, l)
            xb, h2 = out_proj_norm(mixed, w_o, l, xb, ln2_g[l])
            act = swiglu_in(h2, w_ffn_in, l)
            w_out16 = w_ffn_out[l].astype(BF16)
            if l < depth - 1:
                xb = matmul_residual(act, w_out16, xb)
            else:
                xb = matmul_residual_norm(act, w_out16, xb, lnf_g)
        outs.append(xb.reshape(1, s, d))
    return outs[0] if b == 1 else jnp.concatenate(outs, axis=0)
```

```python
import functools

import jax
import jax.numpy as jnp
from jax import lax
from jax.experimental import pallas as pl
from jax.experimental.pallas import tpu as pltpu

F32 = jnp.float32
BF16 = jnp.bfloat16

D_MODEL = 2048
HEAD_DIM = 128
CONV_CH = 512
CONV_WIDTH = 31
CONV_HALF = CONV_WIDTH // 2
B_Q_HEADS = 8
B_KV_HEADS = 2
B_GROUP = B_Q_HEADS // B_KV_HEADS
B_HALF_WINDOW = 128
C_PATTERNS = ((128, 1), (512, 4), (2048, 16))
C_HEADS_PER_GROUP = 4
C_HALF = 64
assert C_PATTERNS[0][1] == 1 and all(w // (2 * r) == C_HALF for w, r in C_PATTERNS)
N_BRANCHES = 3
N_ATTN_HEADS = B_Q_HEADS + len(C_PATTERNS) * C_HEADS_PER_GROUP
ALIBI_MAX_EXP = 8.0
D_FF = -(-8 * D_MODEL // (3 * 256)) * 256
EPS = 1e-6
NEG = -1e30
LOG2E = 1.4426950408889634

LANES = 128
MXU_COLS = 256
SUBLANES = 8
BF16_ROWS = 16
LATTICE_STRIDE = 4
COL = 512
PLAIN_COLS = 5 * COL
QC_BLOCK0 = 5
GATE_BLOCK0 = 14
CONV_TILE = 1024
CONV_HALO = 16
C_TOKENS = 2048
C_BLOCK_Q = 128

VMEM_LIMIT = 56 * 1024 * 1024


def _params(n_axes, flags=None):
    return pltpu.CompilerParams(
        dimension_semantics=("arbitrary",) * n_axes, vmem_limit_bytes=VMEM_LIMIT, flags=flags)


def _cast_rows(src_ref, dst_ref, chunk=256):
    rows = src_ref.shape[0]
    chunk = min(chunk, rows)

    def body(c, carry):
        sl = pl.ds(pl.multiple_of(c * chunk, chunk), chunk)
        dst_ref[sl, :] = src_ref[sl, :].astype(dst_ref.dtype)
        return carry

    lax.fori_loop(0, rows // chunk, body, 0)


def _rms(x, g):
    ms = jnp.mean(x * x, axis=-1, keepdims=True)
    return x * lax.rsqrt(ms + EPS) * g


def _ws_mm_kernel(a_ref, w_ref, o_ref, w16_ref):
    @pl.when(pl.program_id(1) == 0)
    def _():
        _cast_rows(w_ref, w16_ref)

    o_ref[...] = jnp.dot(a_ref[...], w16_ref[...],
                         preferred_element_type=F32).astype(o_ref.dtype)


def proj_plain(h, w_in, layer, tm=1024, tn=PLAIN_COLS // 2):
    s, d = h.shape
    return pl.pallas_call(
        _ws_mm_kernel,
        grid=(PLAIN_COLS // tn, s // tm),
        in_specs=[pl.BlockSpec((tm, d), lambda n, m: (m, 0)),
                  pl.BlockSpec((None, d, tn), lambda n, m: (layer, 0, n))],
        out_specs=pl.BlockSpec((tm, tn), lambda n, m: (m, n)),
        out_shape=jax.ShapeDtypeStruct((s, PLAIN_COLS), BF16),
        scratch_shapes=[pltpu.VMEM((d, tn), BF16)],
        compiler_params=_params(2),
        name="proj_plain",
    )(h, w_in)


def _lattice_kernel(a_ref, wq_ref, wk_ref, wv_ref, o_ref, w16_ref, acc_ref, tmp_ref, *, r):
    @pl.when(pl.program_id(0) == 0)
    def _():
        for j, w_ref in enumerate((wq_ref, wk_ref, wv_ref)):
            _cast_rows(w_ref, w16_ref.at[j])

    a = a_ref[...]
    rows = a.shape[0] // r
    slabs = COL // LANES
    for j in range(3):
        acc = jnp.dot(a, w16_ref[j], preferred_element_type=F32)
        for s in range(slabs):
            acc_ref[j * slabs + s] = acc[:, s * LANES:(s + 1) * LANES]
        for s in range(slabs):
            slab = acc_ref.at[j * slabs + s]
            col = j * COL + s * LANES
            if r == LATTICE_STRIDE:
                for c in range(r):
                    o_ref[c, :, col:col + LANES] = slab[pl.ds(c, rows, stride=r), :].astype(o_ref.dtype)
                continue
            assert r == LATTICE_STRIDE ** 2
            tmp = tmp_ref.at[j * slabs + s]
            for c0 in range(LATTICE_STRIDE):
                tmp[c0] = slab[pl.ds(c0, rows * LATTICE_STRIDE, stride=LATTICE_STRIDE), :]
            for c0 in range(LATTICE_STRIDE):
                for c1 in range(LATTICE_STRIDE):
                    o_ref[c0 + LATTICE_STRIDE * c1, :, col:col + LANES] = (
                        tmp[c0, pl.ds(c1, rows, stride=LATTICE_STRIDE), :].astype(o_ref.dtype))


def _norm_proj_kernel(x_ref, g_ref, wq_ref, wk_ref, wv_ref, o_ref, h_ref, w16_ref):
    @pl.when(pl.program_id(0) == 0)
    def _():
        for j, w_ref in enumerate((wq_ref, wk_ref, wv_ref)):
            _cast_rows(w_ref, w16_ref.at[j])

    half = x_ref.shape[0] // 2
    for rows in (slice(0, half), slice(half, 2 * half)):
        h = _rms(x_ref[rows, :], g_ref[...]).astype(h_ref.dtype)
        h_ref[rows, :] = h
        for j in range(3):
            o_ref[0, rows, j * COL:(j + 1) * COL] = jnp.dot(
                h, w16_ref[j], preferred_element_type=F32).astype(o_ref.dtype)


def norm_proj_undilated(x, g, w_in, layer, tm=1024):
    s, d = x.shape

    def w_spec(j):
        return pl.BlockSpec((None, d, COL), lambda m: (layer, 0, QC_BLOCK0 + 3 * j),
                            pipeline_mode=pl.Buffered(1))

    return pl.pallas_call(
        _norm_proj_kernel,
        grid=(s // tm,),
        in_specs=[pl.BlockSpec((tm, d), lambda m: (m, 0)), pl.BlockSpec((1, d), lambda m: (0, 0)),
                  w_spec(0), w_spec(1), w_spec(2)],
        out_specs=[pl.BlockSpec((1, tm, 3 * COL), lambda m: (0, m, 0)),
                   pl.BlockSpec((tm, d), lambda m: (m, 0))],
        out_shape=[jax.ShapeDtypeStruct((1, s, 3 * COL), BF16), jax.ShapeDtypeStruct((s, d), BF16)],
        scratch_shapes=[pltpu.VMEM((3, d, COL), BF16)],
        compiler_params=_params(1),
        name="norm_proj_undilated",
    )(x, g.reshape(1, d), w_in, w_in, w_in)


def proj_lattice(h, w_in, layer, group, r, tm=1024):
    s, d = h.shape
    assert r in (LATTICE_STRIDE, LATTICE_STRIDE ** 2)
    n_slabs = 3 * COL // LANES
    two_pass_rows = tm // LATTICE_STRIDE if r == LATTICE_STRIDE ** 2 else SUBLANES

    def w_spec(j):
        return pl.BlockSpec((None, d, COL), lambda m: (layer, 0, QC_BLOCK0 + group + 3 * j),
                            pipeline_mode=pl.Buffered(1))

    return pl.pallas_call(
        functools.partial(_lattice_kernel, r=r),
        grid=(s // tm,),
        in_specs=[pl.BlockSpec((tm, d), lambda m: (m, 0)), w_spec(0), w_spec(1), w_spec(2)],
        out_specs=pl.BlockSpec((r, tm // r, 3 * COL), lambda m: (0, m, 0)),
        out_shape=jax.ShapeDtypeStruct((r, s // r, 3 * COL), BF16),
        scratch_shapes=[pltpu.VMEM((3, d, COL), BF16),
                        pltpu.VMEM((n_slabs, tm, LANES), F32),
                        pltpu.VMEM((n_slabs, LATTICE_STRIDE, two_pass_rows, LANES), F32)],
        compiler_params=_params(1),
        name=f"proj_lattice_r{r}",
    )(h, w_in, w_in, w_in)


def _merge_kernel(h_ref, ca_ref, ob_ref, oc_ref, wg0_ref, wg1_ref, wg2_ref,
                  wa_ref, wb_ref, wc_ref, o_ref, wg16_ref, wa16_ref, wb16_ref, wc16_ref):
    @pl.when(pl.program_id(1) == 0)
    def _():
        for b, wg_ref in enumerate((wg0_ref, wg1_ref, wg2_ref)):
            _cast_rows(wg_ref, wg16_ref.at[b])
        _cast_rows(wa_ref, wa16_ref)
        _cast_rows(wb_ref, wb16_ref)
        _cast_rows(wc_ref, wc16_ref)

    h = h_ref[...]
    ys = (ca_ref[...], ob_ref[...], oc_ref[...])
    ws = (wa16_ref, wb16_ref, wc16_ref)
    for c0 in range(0, o_ref.shape[1], MXU_COLS):
        cols = slice(c0, c0 + MXU_COLS)
        mixed = None
        for b in range(N_BRANCHES):
            gate = jax.nn.sigmoid(jnp.dot(h, wg16_ref[b, :, cols], preferred_element_type=F32))
            term = gate * jnp.dot(ys[b], ws[b][:, cols], preferred_element_type=F32)
            mixed = term if mixed is None else mixed + term
        o_ref[:, cols] = mixed.astype(o_ref.dtype)


def gated_merge(h, ca, ob, oc, w_in, w_a, w_b, w_c, layer, tm=1024, tn=COL):
    s, d = h.shape
    nj = d // tn
    widths = (ca.shape[1], ob.shape[1], oc.shape[1])

    def gate_spec(b):
        return pl.BlockSpec((None, d, tn), lambda n, m: (layer, 0, GATE_BLOCK0 + b * nj + n),
                            pipeline_mode=pl.Buffered(1))

    def lhs_spec(width):
        return pl.BlockSpec((tm, width), lambda n, m: (m, 0))

    def w_spec(width):
        return pl.BlockSpec((None, width, tn), lambda n, m: (layer, 0, n),
                            pipeline_mode=pl.Buffered(1))

    return pl.pallas_call(
        _merge_kernel,
        grid=(nj, s // tm),
        in_specs=[lhs_spec(d)] + [lhs_spec(w) for w in widths]
        + [gate_spec(0), gate_spec(1), gate_spec(2)] + [w_spec(w) for w in widths],
        out_specs=pl.BlockSpec((tm, tn), lambda n, m: (m, n)),
        out_shape=jax.ShapeDtypeStruct((s, d), BF16),
        scratch_shapes=[pltpu.VMEM((N_BRANCHES, d, tn), BF16)]
        + [pltpu.VMEM((w, tn), BF16) for w in widths],
        compiler_params=_params(2),
        name="gated_merge",
    )(h, ca, ob, oc, w_in, w_in, w_in, w_a, w_b, w_c)


def _out_proj_kernel(a_ref, w_ref, x_ref, g_ref, xo_ref, ho_ref, w16_ref):
    @pl.when(pl.program_id(0) == 0)
    def _():
        _cast_rows(w_ref, w16_ref)

    half = a_ref.shape[0] // 2
    for rows in (slice(0, half), slice(half, 2 * half)):
        x = x_ref[rows, :] + jnp.dot(a_ref[rows, :], w16_ref[...], preferred_element_type=F32)
        xo_ref[rows, :] = x
        ho_ref[rows, :] = _rms(x, g_ref[...]).astype(ho_ref.dtype)


def out_proj_norm(a, w_o, layer, x, g, tm=512):
    s, k = a.shape
    d = x.shape[1]
    row = pl.BlockSpec((tm, d), lambda m: (m, 0))
    return pl.pallas_call(
        _out_proj_kernel,
        grid=(s // tm,),
        in_specs=[pl.BlockSpec((tm, k), lambda m: (m, 0)),
                  pl.BlockSpec((None, k, d), lambda m: (layer, 0, 0), pipeline_mode=pl.Buffered(1)),
                  row,
                  pl.BlockSpec((1, d), lambda m: (0, 0))],
        out_specs=[row, row],
        out_shape=[jax.ShapeDtypeStruct((s, d), F32), jax.ShapeDtypeStruct((s, d), BF16)],
        scratch_shapes=[pltpu.VMEM((k, d), BF16)],
        compiler_params=_params(1),
        name="out_proj_norm",
    )(a, w_o, x, g.reshape(1, d))


def _swiglu_kernel(h_ref, wg_ref, wu_ref, o_ref, w16_ref):
    @pl.when(pl.program_id(1) == 0)
    def _():
        _cast_rows(wg_ref, w16_ref.at[0])
        _cast_rows(wu_ref, w16_ref.at[1])

    h = h_ref[...]
    for c0 in range(0, o_ref.shape[1], MXU_COLS):
        cols = slice(c0, c0 + MXU_COLS)
        g = jnp.dot(h, w16_ref[0, :, cols], preferred_element_type=F32)
        u = jnp.dot(h, w16_ref[1, :, cols], preferred_element_type=F32)
        o_ref[:, cols] = (jax.nn.silu(g) * u).astype(o_ref.dtype)


def swiglu_in(h, w_ffn_in, layer, tm=2048, tn=COL):
    s, d = h.shape
    nj = D_FF // tn
    return pl.pallas_call(
        _swiglu_kernel,
        grid=(nj, s // tm),
        in_specs=[pl.BlockSpec((tm, d), lambda n, m: (m, 0)),
                  pl.BlockSpec((None, d, tn), lambda n, m: (layer, 0, n)),
                  pl.BlockSpec((None, d, tn), lambda n, m: (layer, 0, nj + n))],
        out_specs=pl.BlockSpec((tm, tn), lambda n, m: (m, n)),
        out_shape=jax.ShapeDtypeStruct((s, D_FF), BF16),
        scratch_shapes=[pltpu.VMEM((2, d, tn), BF16)],
        compiler_params=_params(2),
        name="swiglu_in",
    )(h, w_ffn_in, w_ffn_in)


def _mm_residual_kernel(a_ref, w_ref, x_ref, o_ref):
    o_ref[...] = x_ref[...] + jnp.dot(a_ref[...], w_ref[...], preferred_element_type=F32)


def matmul_residual(a, w16, x, tm=1024, tn=COL):
    s, k = a.shape
    d = w16.shape[1]
    return pl.pallas_call(
        _mm_residual_kernel,
        grid=(s // tm, d // tn),
        in_specs=[pl.BlockSpec((tm, k), lambda i, j: (i, 0)),
                  pl.BlockSpec((k, tn), lambda i, j: (0, j)),
                  pl.BlockSpec((tm, tn), lambda i, j: (i, j))],
        out_specs=pl.BlockSpec((tm, tn), lambda i, j: (i, j)),
        out_shape=jax.ShapeDtypeStruct((s, d), F32),
        compiler_params=_params(2),
        name="matmul_residual",
    )(a, w16, x)


def _mm_residual_norm_kernel(a_ref, w_ref, x_ref, g_ref, o_ref, ss_ref):
    n = pl.program_id(1)
    tn = x_ref.shape[1]
    x = x_ref[...] + jnp.dot(a_ref[...], w_ref[...], preferred_element_type=F32)
    ss = jnp.sum(x * x, axis=-1, keepdims=True)

    @pl.when(n == 0)
    def _():
        ss_ref[...] = jnp.broadcast_to(ss, ss_ref.shape)

    @pl.when(n > 0)
    def _():
        ss_ref[...] += ss

    for j in range(o_ref.shape[1] // tn):
        @pl.when(n == j)
        def _():
            o_ref[:, j * tn:(j + 1) * tn] = x

    @pl.when(n == pl.num_programs(1) - 1)
    def _():
        ms = ss_ref[:, 0:1] * (1.0 / o_ref.shape[1])
        o_ref[...] = o_ref[...] * lax.rsqrt(ms + EPS) * g_ref[...]


def matmul_residual_norm(a, w16, x, g, tm=1024, tn=COL):
    s, k = a.shape
    d = w16.shape[1]
    return pl.pallas_call(
        _mm_residual_norm_kernel,
        grid=(s // tm, d // tn),
        in_specs=[pl.BlockSpec((tm, k), lambda i, j: (i, 0)),
                  pl.BlockSpec((k, tn), lambda i, j: (0, j)),
                  pl.BlockSpec((tm, tn), lambda i, j: (i, j)),
                  pl.BlockSpec((1, d), lambda i, j: (0, 0))],
        out_specs=pl.BlockSpec((tm, d), lambda i, j: (i, 0)),
        out_shape=jax.ShapeDtypeStruct((s, d), F32),
        scratch_shapes=[pltpu.VMEM((tm, LANES), F32)],
        compiler_params=_params(2),
        name="matmul_residual_norm",
    )(a, w16, x, g.reshape(1, d))


def _conv_tile(up_ref, uc_ref, un_ref, w_ref, b_ref, g_ref, beta_ref, o_ref, z_ref, sh_ref,
               *, first, last, chunk=64):
    tm, halo = CONV_TILE, CONV_HALO

    def glu(u):
        u = u.astype(F32)
        return u[:, :CONV_CH] * jax.nn.sigmoid(u[:, CONV_CH:])

    z_ref[0:halo] = jnp.where(first, 0.0, glu(up_ref[...]))
    z_ref[halo:halo + tm] = glu(uc_ref[...])
    z_ref[halo + tm:2 * halo + tm] = jnp.where(last, 0.0, glu(un_ref[...]))
    span = sh_ref.shape[1]
    for j in range(1, SUBLANES):
        sh_ref[j - 1] = z_ref[j:j + span, :]

    for rc in range(tm // chunk):
        acc = jnp.zeros((chunk, CONV_CH), F32)
        for k in range(CONV_WIDTH):
            off = rc * chunk + halo - CONV_HALF + k
            j = off % SUBLANES
            src = z_ref if j == 0 else sh_ref.at[j - 1]
            acc = acc + src[off - j:off - j + chunk, :] * w_ref[k:k + 1, :]
        z = acc + b_ref[...]
        mu = jnp.mean(z, axis=-1, keepdims=True)
        zc = z - mu
        var = jnp.mean(zc * zc, axis=-1, keepdims=True)
        y = zc * lax.rsqrt(var + EPS) * g_ref[...] + beta_ref[...]
        o_ref[rc * chunk:(rc + 1) * chunk, :] = jax.nn.silu(y).astype(o_ref.dtype)


def _conv_kernel(up_ref, uc_ref, un_ref, w_ref, b_ref, g_ref, beta_ref, wo_ref,
                 o_ref, wo16_ref, z_ref, sh_ref):
    i = pl.program_id(0)
    _conv_tile(up_ref, uc_ref, un_ref, w_ref, b_ref, g_ref, beta_ref, o_ref, z_ref, sh_ref,
               first=i == 0, last=i == pl.num_programs(0) - 1)
    wo16_ref[...] = wo_ref[...].astype(wo16_ref.dtype)


def conv_module(plain, conv_w, conv_b, cnorm_g, cnorm_b, w_ffn_out, layer):
    s = plain.shape[0]
    steps = s // CONV_TILE
    nh = CONV_TILE // CONV_HALO
    width = 2 * CONV_CH
    _, d_ff, d_out = w_ffn_out.shape
    wo_rows = d_ff // steps
    assert wo_rows * steps == d_ff and wo_rows % BF16_ROWS == 0
    vec_spec = pl.BlockSpec((None, 1, CONV_CH), lambda i: (layer, 0, 0))
    row = lambda v: v.reshape(v.shape[0], 1, CONV_CH)
    return pl.pallas_call(
        _conv_kernel,
        grid=(steps,),
        in_specs=[pl.BlockSpec((CONV_HALO, width), lambda i: (jnp.maximum(i * nh - 1, 0), 0)),
                  pl.BlockSpec((CONV_TILE, width), lambda i: (i, 0)),
                  pl.BlockSpec((CONV_HALO, width),
                               lambda i: (jnp.minimum((i + 1) * nh, s // CONV_HALO - 1), 0)),
                  pl.BlockSpec((None, CONV_WIDTH, CONV_CH), lambda i: (layer, 0, 0)),
                  vec_spec, vec_spec, vec_spec,
                  pl.BlockSpec((None, wo_rows, d_out), lambda i: (layer, i, 0))],
        out_specs=[pl.BlockSpec((CONV_TILE, CONV_CH), lambda i: (i, 0)),
                   pl.BlockSpec((wo_rows, d_out), lambda i: (i, 0))],
        out_shape=[jax.ShapeDtypeStruct((s, CONV_CH), BF16),
                   jax.ShapeDtypeStruct((d_ff, d_out), BF16)],
        scratch_shapes=[pltpu.VMEM((CONV_TILE + 2 * CONV_HALO, CONV_CH), F32),
                        pltpu.VMEM((SUBLANES - 1, CONV_TILE + 2 * CONV_HALO - SUBLANES, CONV_CH), F32)],
        compiler_params=_params(1),
        name="conv_module",
    )(plain, plain, plain, conv_w, row(conv_b), row(cnorm_g), row(cnorm_b), w_ffn_out)


def _band_tables(bq, half, step):
    shape = (bq, bq + 2 * half)
    qi = lax.broadcasted_iota(jnp.int32, shape, 0)
    kj = lax.broadcasted_iota(jnp.int32, shape, 1)
    rel = jnp.abs(kj - half - qi)
    return rel <= half, (step * rel).astype(F32), kj


def _edge_valid(band, kj, bq, half, blk, n_blocks, first, last):
    valid = band
    if first:
        valid = valid & (kj >= jnp.where(blk > 0, 0, half))
    if last:
        valid = valid & (kj < jnp.where(blk < n_blocks - 1, bq + 2 * half, bq + half))
    return valid


def _scores(q, k):
    return lax.dot_general(q, k, (((1,), (1,)), ((), ())), preferred_element_type=F32)


def _softmax_rows(sc, bias2, valid, sink2):
    t = jnp.where(valid, sc * (HEAD_DIM ** -0.5 * LOG2E) + bias2, NEG * LOG2E)
    m2 = jnp.max(t, axis=-1, keepdims=True)
    if sink2 is not None:
        m2 = jnp.maximum(m2, sink2)
    p = jnp.exp2(t - m2)
    denom = jnp.sum(p, axis=-1, keepdims=True)
    if sink2 is not None:
        denom = denom + jnp.exp2(sink2 - m2)
    return p, m2, denom


def _swa_kernel(slopes_ref, sink_ref, q_ref, kp_ref, kc_ref, kn_ref, vp_ref, vc_ref, vn_ref, o_ref,
                *, tq, half, n_blocks, layer):
    i = pl.program_id(0)
    hk = pl.program_id(1)
    bq = half
    nsub = tq // bq
    kcat = jnp.concatenate([kp_ref[...], kc_ref[...], kn_ref[...]], axis=0)
    vcat = jnp.concatenate([vp_ref[...], vc_ref[...], vn_ref[...]], axis=0)
    band, dist, kj = _band_tables(bq, half, 1)
    heads = [hk * B_GROUP + g for g in range(B_GROUP)]
    biases = [(-LOG2E * slopes_ref[hd]) * dist for hd in heads]
    sinks = [LOG2E * sink_ref[layer, hd] for hd in heads]
    for s in range(nsub):
        rows = slice(s * bq, (s + 1) * bq)
        win = slice(s * bq, (s + 1) * bq + 2 * half)
        valid = _edge_valid(band, kj, bq, half, i * nsub + s, n_blocks, s == 0, s == nsub - 1)
        q_all = jnp.concatenate(
            [q_ref[rows, g * HEAD_DIM:(g + 1) * HEAD_DIM] for g in range(B_GROUP)], axis=0)
        sc_all = _scores(q_all, kcat[win])
        ps, denoms = [], []
        for g in range(B_GROUP):
            p, _, denom = _softmax_rows(sc_all[g * bq:(g + 1) * bq], biases[g], valid, sinks[g])
            ps.append(p.astype(BF16))
            denoms.append(denom)
        o_all = jnp.dot(jnp.concatenate(ps, axis=0), vcat[win], preferred_element_type=F32)
        for g in range(B_GROUP):
            o = o_all[g * bq:(g + 1) * bq] / denoms[g]
            o_ref[rows, g * HEAD_DIM:(g + 1) * HEAD_DIM] = o.astype(o_ref.dtype)


def windowed_gqa(plain, slopes, sink, layer, tq=1024):
    s = plain.shape[0]
    half = B_HALF_WINDOW
    nh = tq // half
    qw = B_GROUP * HEAD_DIM
    q_blk0 = 2 * CONV_CH // qw
    k_blk0 = (2 * CONV_CH + B_Q_HEADS * HEAD_DIM) // HEAD_DIM
    v_blk0 = k_blk0 + B_KV_HEADS

    def halo_specs(blk0):
        return [pl.BlockSpec((half, HEAD_DIM), lambda i, h: (jnp.maximum(i * nh - 1, 0), blk0 + h)),
                pl.BlockSpec((tq, HEAD_DIM), lambda i, h: (i, blk0 + h)),
                pl.BlockSpec((half, HEAD_DIM),
                             lambda i, h: (jnp.minimum((i + 1) * nh, s // half - 1), blk0 + h))]

    smem = pl.BlockSpec(memory_space=pltpu.SMEM)
    return pl.pallas_call(
        functools.partial(_swa_kernel, tq=tq, half=half, n_blocks=s // half, layer=layer),
        grid=(s // tq, B_KV_HEADS),
        in_specs=[smem, smem, pl.BlockSpec((tq, qw), lambda i, h: (i, q_blk0 + h))]
        + halo_specs(k_blk0) + halo_specs(v_blk0),
        out_specs=pl.BlockSpec((tq, qw), lambda i, h: (i, h)),
        out_shape=jax.ShapeDtypeStruct((s, B_Q_HEADS * HEAD_DIM), BF16),
        compiler_params=_params(2),
        name="windowed_gqa",
    )(slopes, sink, plain, plain, plain, plain, plain, plain, plain)


def _dilated_kernel(slopes_ref, *refs, tokens, bq, half, seq_len):
    o_ref, u_scr, m_scr, d_scr = refs[-4:]
    i = pl.program_id(0)
    h = pl.program_id(1)
    for gi, (_, r) in enumerate(C_PATTERNS):
        q_ref, kp_ref, kc_ref, kn_ref, vp_ref, vc_ref, vn_ref = refs[7 * gi:7 * gi + 7]
        rows = tokens // r
        nsub = rows // bq
        n_blocks = seq_len // r // bq
        band, dist, kj = _band_tables(bq, half, r)
        bias = (-LOG2E * slopes_ref[B_Q_HEADS + gi * C_HEADS_PER_GROUP + h]) * dist
        for c in range(r):
            kcat = jnp.concatenate([kp_ref[c], kc_ref[c], kn_ref[c]], axis=0)
            vcat = jnp.concatenate([vp_ref[c], vc_ref[c], vn_ref[c]], axis=0)
            for s in range(nsub):
                win = slice(s * bq, (s + 1) * bq + 2 * half)
                valid = _edge_valid(band, kj, bq, half, i * nsub + s, n_blocks, s == 0, s == nsub - 1)
                p, m2, denom = _softmax_rows(_scores(q_ref[c, s * bq:(s + 1) * bq, :], kcat[win]),
                                             bias, valid, None)
                dst = pl.ds(s * bq * r + c, bq, stride=r) if r > 1 else pl.ds(s * bq, bq)
                u_scr[gi, dst, :] = jnp.dot(p.astype(BF16), vcat[win], preferred_element_type=F32)
                m_scr[gi, dst, :] = jnp.broadcast_to(m2, (bq, HEAD_DIM))
                d_scr[gi, dst, :] = jnp.broadcast_to(denom, (bq, HEAD_DIM))
    m_max = jnp.maximum(jnp.maximum(m_scr[0], m_scr[1]), m_scr[2])
    num = den = None
    for gi in range(len(C_PATTERNS)):
        w = jnp.exp2(m_scr[gi] - m_max)
        num = w * u_scr[gi] if num is None else num + w * u_scr[gi]
        den = w * d_scr[gi] if den is None else den + w * d_scr[gi]
    o_ref[...] = (num / den).astype(o_ref.dtype)


def dilated_attention(lattices, slopes, seq_len, tokens=C_TOKENS, bq=C_BLOCK_Q, half=C_HALF):
    nq = C_HEADS_PER_GROUP
    in_specs = [pl.BlockSpec(memory_space=pltpu.SMEM)]
    args = [slopes]
    for (_, r), lat in zip(C_PATTERNS, lattices):
        rows = tokens // r
        nh = rows // half
        last = seq_len // r // half - 1

        def specs(blk0, r=r, rows=rows, nh=nh, last=last):
            return [pl.BlockSpec((r, half, HEAD_DIM), lambda i, h: (0, jnp.maximum(i * nh - 1, 0), blk0 + h)),
                    pl.BlockSpec((r, rows, HEAD_DIM), lambda i, h: (0, i, blk0 + h)),
                    pl.BlockSpec((r, half, HEAD_DIM), lambda i, h: (0, jnp.minimum((i + 1) * nh, last), blk0 + h))]

        in_specs += [pl.BlockSpec((r, rows, HEAD_DIM), lambda i, h: (0, i, h))] + specs(nq) + specs(2 * nq)
        args += [lat] * 7
    return pl.pallas_call(
        functools.partial(_dilated_kernel, tokens=tokens, bq=bq, half=half, seq_len=seq_len),
        grid=(seq_len // tokens, nq),
        in_specs=in_specs,
        out_specs=pl.BlockSpec((tokens, HEAD_DIM), lambda i, h: (i, h)),
        out_shape=jax.ShapeDtypeStruct((seq_len, nq * HEAD_DIM), BF16),
        scratch_shapes=[pltpu.VMEM((len(C_PATTERNS), tokens, HEAD_DIM), F32)] * 3,
        compiler_params=_params(2),
        name="dilated_attention",
    )(*args)


def _alibi_slopes():
    h = jnp.arange(1, N_ATTN_HEADS + 1, dtype=F32)
    return jnp.exp2(-ALIBI_MAX_EXP * h / N_ATTN_HEADS)


def kernel(x, ln1_g, w_in, conv_w, conv_b, cnorm_g, cnorm_b, w_a, sink, w_b, w_c, w_o, ln2_g,
           w_ffn_in, w_ffn_out, lnf_g):
    b, s, d = x.shape
    depth = ln1_g.shape[0]
    assert depth >= 1
    slopes = _alibi_slopes()
    outs = []
    for xb in jnp.split(x, b, axis=0):
        xb = xb.reshape(s, d)
        for l in range(depth):
            lat0, h = norm_proj_undilated(xb, ln1_g[l], w_in, l)
            plain = proj_plain(h, w_in, l)
            lattices = [lat0] + [proj_lattice(h, w_in, l, gi, r)
                                 for gi, (_, r) in enumerate(C_PATTERNS) if gi > 0]
            ca, w_out16 = conv_module(plain, conv_w, conv_b, cnorm_g, cnorm_b, w_ffn_out, l)
            ob = windowed_gqa(plain, slopes, sink, l)
            oc = dilated_attention(lattices, slopes, s)
            mixed = gated_merge(h, ca, ob, oc, w_in, w_a, w_b, w_c, l)
            xb, h2 = out_proj_norm(mixed, w_o, l, xb, ln2_g[l])
            act = swiglu_in(h2, w_ffn_in, l)
            if l < depth - 1:
                xb = matmul_residual(act, w_out16, xb)
            else:
                xb = matmul_residual_norm(act, w_out16, xb, lnf_g)
        outs.append(xb.reshape(1, s, d))
    return outs[0] if b == 1 else jnp.concatenate(outs, axis=0)
```

```python
import functools

import jax
import jax.numpy as jnp
from jax import lax
from jax.experimental import pallas as pl
from jax.experimental.pallas import tpu as pltpu

F32 = jnp.float32
BF16 = jnp.bfloat16

D_MODEL = 2048
HEAD_DIM = 128
CONV_CH = 512
CONV_WIDTH = 31
CONV_HALF = CONV_WIDTH // 2
B_Q_HEADS = 8
B_KV_HEADS = 2
B_GROUP = B_Q_HEADS // B_KV_HEADS
B_HALF_WINDOW = 128
C_PATTERNS = ((128, 1), (512, 4), (2048, 16))
C_HEADS_PER_GROUP = 4
C_HALF = 64
assert C_PATTERNS[0][1] == 1 and all(w // (2 * r) == C_HALF for w, r in C_PATTERNS)
N_BRANCHES = 3
N_ATTN_HEADS = B_Q_HEADS + len(C_PATTERNS) * C_HEADS_PER_GROUP
ALIBI_MAX_EXP = 8.0
D_FF = -(-8 * D_MODEL // (3 * 256)) * 256
EPS = 1e-6
NEG = -1e30
LOG2E = 1.4426950408889634

LANES = 128
MXU_COLS = 256
SUBLANES = 8
BF16_ROWS = 16
LATTICE_STRIDE = 4
COL = 512
PLAIN_COLS = 5 * COL
QC_BLOCK0 = 5
GATE_BLOCK0 = 14
N_GATE_BLOCKS = N_BRANCHES * D_MODEL // COL
CONV_TILE = 1024
CONV_HALO = 16
C_TOKENS = 2048
C_BLOCK_Q = 128

VMEM_LIMIT = 56 * 1024 * 1024


def _params(n_axes, flags=None):
    return pltpu.CompilerParams(
        dimension_semantics=("arbitrary",) * n_axes, vmem_limit_bytes=VMEM_LIMIT, flags=flags)


def _cast_rows(src_ref, dst_ref, chunk=256):
    rows = src_ref.shape[0]
    chunk = min(chunk, rows)

    def body(c, carry):
        sl = pl.ds(pl.multiple_of(c * chunk, chunk), chunk)
        dst_ref[sl, :] = src_ref[sl, :].astype(dst_ref.dtype)
        return carry

    lax.fori_loop(0, rows // chunk, body, 0)


def _rms(x, g):
    ms = jnp.mean(x * x, axis=-1, keepdims=True)
    return x * lax.rsqrt(ms + EPS) * g


def _ws_mm_kernel(a_ref, w_ref, o_ref, w16_ref):
    @pl.when(pl.program_id(1) == 0)
    def _():
        _cast_rows(w_ref, w16_ref)

    o_ref[...] = jnp.dot(a_ref[...], w16_ref[...],
                         preferred_element_type=F32).astype(o_ref.dtype)


def proj_plain(h, w_in, layer, tm=1024, tn=PLAIN_COLS // 2):
    s, d = h.shape
    return pl.pallas_call(
        _ws_mm_kernel,
        grid=(PLAIN_COLS // tn, s // tm),
        in_specs=[pl.BlockSpec((tm, d), lambda n, m: (m, 0)),
                  pl.BlockSpec((None, d, tn), lambda n, m: (layer, 0, n))],
        out_specs=pl.BlockSpec((tm, tn), lambda n, m: (m, n)),
        out_shape=jax.ShapeDtypeStruct((s, PLAIN_COLS), BF16),
        scratch_shapes=[pltpu.VMEM((d, tn), BF16)],
        compiler_params=_params(2),
        name="proj_plain",
    )(h, w_in)


def _lattice_kernel(a_ref, wq_ref, wk_ref, wv_ref, o_ref, w16_ref, acc_ref, tmp_ref, *, r):
    @pl.when(pl.program_id(0) == 0)
    def _():
        for j, w_ref in enumerate((wq_ref, wk_ref, wv_ref)):
            _cast_rows(w_ref, w16_ref.at[j])

    a = a_ref[...]
    rows = a.shape[0] // r
    slabs = COL // LANES
    for j in range(3):
        acc = jnp.dot(a, w16_ref[j], preferred_element_type=F32)
        for s in range(slabs):
            acc_ref[j * slabs + s] = acc[:, s * LANES:(s + 1) * LANES]
        for s in range(slabs):
            slab = acc_ref.at[j * slabs + s]
            col = j * COL + s * LANES
            if r == LATTICE_STRIDE:
                for c in range(r):
                    o_ref[c, :, col:col + LANES] = slab[pl.ds(c, rows, stride=r), :].astype(o_ref.dtype)
                continue
            assert r == LATTICE_STRIDE ** 2
            tmp = tmp_ref.at[j * slabs + s]
            for c0 in range(LATTICE_STRIDE):
                tmp[c0] = slab[pl.ds(c0, rows * LATTICE_STRIDE, stride=LATTICE_STRIDE), :]
            for c0 in range(LATTICE_STRIDE):
                for c1 in range(LATTICE_STRIDE):
                    o_ref[c0 + LATTICE_STRIDE * c1, :, col:col + LANES] = (
                        tmp[c0, pl.ds(c1, rows, stride=LATTICE_STRIDE), :].astype(o_ref.dtype))


def _norm_proj_kernel(x_ref, g_ref, wq_ref, wk_ref, wv_ref, o_ref, h_ref, w16_ref):
    @pl.when(pl.program_id(0) == 0)
    def _():
        for j, w_ref in enumerate((wq_ref, wk_ref, wv_ref)):
            _cast_rows(w_ref, w16_ref.at[j])

    half = x_ref.shape[0] // 2
    for rows in (slice(0, half), slice(half, 2 * half)):
        h = _rms(x_ref[rows, :], g_ref[...]).astype(h_ref.dtype)
        h_ref[rows, :] = h
        for j in range(3):
            o_ref[0, rows, j * COL:(j + 1) * COL] = jnp.dot(
                h, w16_ref[j], preferred_element_type=F32).astype(o_ref.dtype)


def norm_proj_undilated(x, g, w_in, layer, tm=1024):
    s, d = x.shape

    def w_spec(j):
        return pl.BlockSpec((None, d, COL), lambda m: (layer, 0, QC_BLOCK0 + 3 * j),
                            pipeline_mode=pl.Buffered(1))

    return pl.pallas_call(
        _norm_proj_kernel,
        grid=(s // tm,),
        in_specs=[pl.BlockSpec((tm, d), lambda m: (m, 0)), pl.BlockSpec((1, d), lambda m: (0, 0)),
                  w_spec(0), w_spec(1), w_spec(2)],
        out_specs=[pl.BlockSpec((1, tm, 3 * COL), lambda m: (0, m, 0)),
                   pl.BlockSpec((tm, d), lambda m: (m, 0))],
        out_shape=[jax.ShapeDtypeStruct((1, s, 3 * COL), BF16), jax.ShapeDtypeStruct((s, d), BF16)],
        scratch_shapes=[pltpu.VMEM((3, d, COL), BF16)],
        compiler_params=_params(1),
        name="norm_proj_undilated",
    )(x, g.reshape(1, d), w_in, w_in, w_in)


def proj_lattice(h, w_in, layer, group, r, tm=1024):
    s, d = h.shape
    assert r in (LATTICE_STRIDE, LATTICE_STRIDE ** 2)
    n_slabs = 3 * COL // LANES
    two_pass_rows = tm // LATTICE_STRIDE if r == LATTICE_STRIDE ** 2 else SUBLANES

    def w_spec(j):
        return pl.BlockSpec((None, d, COL), lambda m: (layer, 0, QC_BLOCK0 + group + 3 * j),
                            pipeline_mode=pl.Buffered(1))

    return pl.pallas_call(
        functools.partial(_lattice_kernel, r=r),
        grid=(s // tm,),
        in_specs=[pl.BlockSpec((tm, d), lambda m: (m, 0)), w_spec(0), w_spec(1), w_spec(2)],
        out_specs=pl.BlockSpec((r, tm // r, 3 * COL), lambda m: (0, m, 0)),
        out_shape=jax.ShapeDtypeStruct((r, s // r, 3 * COL), BF16),
        scratch_shapes=[pltpu.VMEM((3, d, COL), BF16),
                        pltpu.VMEM((n_slabs, tm, LANES), F32),
                        pltpu.VMEM((n_slabs, LATTICE_STRIDE, two_pass_rows, LANES), F32)],
        compiler_params=_params(1),
        name=f"proj_lattice_r{r}",
    )(h, w_in, w_in, w_in)


def _merge_kernel(h_ref, ca_ref, ob_ref, oc_ref, wg0_ref, wg1_ref, wg2_ref,
                  wa_ref, wb_ref, wc_ref, o_ref, wa16_ref, wb16_ref, wc16_ref):
    @pl.when(pl.program_id(1) == 0)
    def _():
        _cast_rows(wa_ref, wa16_ref)
        _cast_rows(wb_ref, wb16_ref)
        _cast_rows(wc_ref, wc16_ref)

    h = h_ref[...]
    ys = (ca_ref[...], ob_ref[...], oc_ref[...])
    ws = (wa16_ref, wb16_ref, wc16_ref)
    wgs = (wg0_ref, wg1_ref, wg2_ref)
    for c0 in range(0, o_ref.shape[1], MXU_COLS):
        cols = slice(c0, c0 + MXU_COLS)
        mixed = None
        for b in range(N_BRANCHES):
            gate = jax.nn.sigmoid(jnp.dot(h, wgs[b][:, cols], preferred_element_type=F32))
            term = gate * jnp.dot(ys[b], ws[b][:, cols], preferred_element_type=F32)
            mixed = term if mixed is None else mixed + term
        o_ref[:, cols] = mixed.astype(o_ref.dtype)


def gated_merge(h, ca, ob, oc, gates16, w_a, w_b, w_c, layer, tm=1024, tn=COL):
    s, d = h.shape
    nj = d // tn
    widths = (ca.shape[1], ob.shape[1], oc.shape[1])

    def gate_spec(b):
        return pl.BlockSpec((d, tn), lambda n, m: (0, b * nj + n))

    def lhs_spec(width):
        return pl.BlockSpec((tm, width), lambda n, m: (m, 0))

    def w_spec(width):
        return pl.BlockSpec((None, width, tn), lambda n, m: (layer, 0, n),
                            pipeline_mode=pl.Buffered(1))

    return pl.pallas_call(
        _merge_kernel,
        grid=(nj, s // tm),
        in_specs=[lhs_spec(d)] + [lhs_spec(w) for w in widths]
        + [gate_spec(0), gate_spec(1), gate_spec(2)] + [w_spec(w) for w in widths],
        out_specs=pl.BlockSpec((tm, tn), lambda n, m: (m, n)),
        out_shape=jax.ShapeDtypeStruct((s, d), BF16),
        scratch_shapes=[pltpu.VMEM((w, tn), BF16) for w in widths],
        compiler_params=_params(2),
        name="gated_merge",
    )(h, ca, ob, oc, gates16, gates16, gates16, w_a, w_b, w_c)


def _out_proj_kernel(a_ref, w_ref, x_ref, g_ref, xo_ref, ho_ref, w16_ref):
    @pl.when(pl.program_id(0) == 0)
    def _():
        _cast_rows(w_ref, w16_ref)

    half = a_ref.shape[0] // 2
    for rows in (slice(0, half), slice(half, 2 * half)):
        x = x_ref[rows, :] + jnp.dot(a_ref[rows, :], w16_ref[...], preferred_element_type=F32)
        xo_ref[rows, :] = x
        ho_ref[rows, :] = _rms(x, g_ref[...]).astype(ho_ref.dtype)


def out_proj_norm(a, w_o, layer, x, g, tm=512):
    s, k = a.shape
    d = x.shape[1]
    row = pl.BlockSpec((tm, d), lambda m: (m, 0))
    return pl.pallas_call(
        _out_proj_kernel,
        grid=(s // tm,),
        in_specs=[pl.BlockSpec((tm, k), lambda m: (m, 0)),
                  pl.BlockSpec((None, k, d), lambda m: (layer, 0, 0), pipeline_mode=pl.Buffered(1)),
                  row,
                  pl.BlockSpec((1, d), lambda m: (0, 0))],
        out_specs=[row, row],
        out_shape=[jax.ShapeDtypeStruct((s, d), F32), jax.ShapeDtypeStruct((s, d), BF16)],
        scratch_shapes=[pltpu.VMEM((k, d), BF16)],
        compiler_params=_params(1),
        name="out_proj_norm",
    )(a, w_o, x, g.reshape(1, d))


def _swiglu_kernel(h_ref, wg_ref, wu_ref, o_ref, w16_ref):
    @pl.when(pl.program_id(1) == 0)
    def _():
        _cast_rows(wg_ref, w16_ref.at[0])
        _cast_rows(wu_ref, w16_ref.at[1])

    h = h_ref[...]
    for c0 in range(0, o_ref.shape[1], MXU_COLS):
        cols = slice(c0, c0 + MXU_COLS)
        g = jnp.dot(h, w16_ref[0, :, cols], preferred_element_type=F32)
        u = jnp.dot(h, w16_ref[1, :, cols], preferred_element_type=F32)
        o_ref[:, cols] = (jax.nn.silu(g) * u).astype(o_ref.dtype)


def swiglu_in(h, w_ffn_in, layer, tm=2048, tn=COL):
    s, d = h.shape
    nj = D_FF // tn
    return pl.pallas_call(
        _swiglu_kernel,
        grid=(nj, s // tm),
        in_specs=[pl.BlockSpec((tm, d), lambda n, m: (m, 0)),
                  pl.BlockSpec((None, d, tn), lambda n, m: (layer, 0, n)),
                  pl.BlockSpec((None, d, tn), lambda n, m: (layer, 0, nj + n))],
        out_specs=pl.BlockSpec((tm, tn), lambda n, m: (m, n)),
        out_shape=jax.ShapeDtypeStruct((s, D_FF), BF16),
        scratch_shapes=[pltpu.VMEM((2, d, tn), BF16)],
        compiler_params=_params(2),
        name="swiglu_in",
    )(h, w_ffn_in, w_ffn_in)


def _mm_residual_kernel(a_ref, w_ref, x_ref, o_ref):
    o_ref[...] = x_ref[...] + jnp.dot(a_ref[...], w_ref[...], preferred_element_type=F32)


def matmul_residual(a, w16, x, tm=1024, tn=COL):
    s, k = a.shape
    d = w16.shape[1]
    return pl.pallas_call(
        _mm_residual_kernel,
        grid=(s // tm, d // tn),
        in_specs=[pl.BlockSpec((tm, k), lambda i, j: (i, 0)),
                  pl.BlockSpec((k, tn), lambda i, j: (0, j)),
                  pl.BlockSpec((tm, tn), lambda i, j: (i, j))],
        out_specs=pl.BlockSpec((tm, tn), lambda i, j: (i, j)),
        out_shape=jax.ShapeDtypeStruct((s, d), F32),
        compiler_params=_params(2),
        name="matmul_residual",
    )(a, w16, x)


def _mm_residual_norm_kernel(a_ref, w_ref, x_ref, g_ref, o_ref, ss_ref):
    n = pl.program_id(1)
    tn = x_ref.shape[1]
    x = x_ref[...] + jnp.dot(a_ref[...], w_ref[...], preferred_element_type=F32)
    ss = jnp.sum(x * x, axis=-1, keepdims=True)

    @pl.when(n == 0)
    def _():
        ss_ref[...] = jnp.broadcast_to(ss, ss_ref.shape)

    @pl.when(n > 0)
    def _():
        ss_ref[...] += ss

    for j in range(o_ref.shape[1] // tn):
        @pl.when(n == j)
        def _():
            o_ref[:, j * tn:(j + 1) * tn] = x

    @pl.when(n == pl.num_programs(1) - 1)
    def _():
        ms = ss_ref[:, 0:1] * (1.0 / o_ref.shape[1])
        o_ref[...] = o_ref[...] * lax.rsqrt(ms + EPS) * g_ref[...]


def matmul_residual_norm(a, w16, x, g, tm=1024, tn=COL):
    s, k = a.shape
    d = w16.shape[1]
    return pl.pallas_call(
        _mm_residual_norm_kernel,
        grid=(s // tm, d // tn),
        in_specs=[pl.BlockSpec((tm, k), lambda i, j: (i, 0)),
                  pl.BlockSpec((k, tn), lambda i, j: (0, j)),
                  pl.BlockSpec((tm, tn), lambda i, j: (i, j)),
                  pl.BlockSpec((1, d), lambda i, j: (0, 0))],
        out_specs=pl.BlockSpec((tm, d), lambda i, j: (i, 0)),
        out_shape=jax.ShapeDtypeStruct((s, d), F32),
        scratch_shapes=[pltpu.VMEM((tm, LANES), F32)],
        compiler_params=_params(2),
        name="matmul_residual_norm",
    )(a, w16, x, g.reshape(1, d))


def _conv_tile(up_ref, uc_ref, un_ref, w_ref, b_ref, g_ref, beta_ref, o_ref, z_ref, sh_ref,
               *, first, last, chunk=64):
    tm, halo = CONV_TILE, CONV_HALO

    def glu(u):
        u = u.astype(F32)
        return u[:, :CONV_CH] * jax.nn.sigmoid(u[:, CONV_CH:])

    z_ref[0:halo] = jnp.where(first, 0.0, glu(up_ref[...]))
    z_ref[halo:halo + tm] = glu(uc_ref[...])
    z_ref[halo + tm:2 * halo + tm] = jnp.where(last, 0.0, glu(un_ref[...]))
    span = sh_ref.shape[1]
    for j in range(1, SUBLANES):
        sh_ref[j - 1] = z_ref[j:j + span, :]

    for rc in range(tm // chunk):
        acc = jnp.zeros((chunk, CONV_CH), F32)
        for k in range(CONV_WIDTH):
            off = rc * chunk + halo - CONV_HALF + k
            j = off % SUBLANES
            src = z_ref if j == 0 else sh_ref.at[j - 1]
            acc = acc + src[off - j:off - j + chunk, :] * w_ref[k:k + 1, :]
        z = acc + b_ref[...]
        mu = jnp.mean(z, axis=-1, keepdims=True)
        zc = z - mu
        var = jnp.mean(zc * zc, axis=-1, keepdims=True)
        y = zc * lax.rsqrt(var + EPS) * g_ref[...] + beta_ref[...]
        o_ref[rc * chunk:(rc + 1) * chunk, :] = jax.nn.silu(y).astype(o_ref.dtype)


def _conv_kernel(up_ref, uc_ref, un_ref, w_ref, b_ref, g_ref, beta_ref, wo_ref,
                 o_ref, wo16_ref, z_ref, sh_ref):
    i = pl.program_id(0)
    _conv_tile(up_ref, uc_ref, un_ref, w_ref, b_ref, g_ref, beta_ref, o_ref, z_ref, sh_ref,
               first=i == 0, last=i == pl.num_programs(0) - 1)
    wo16_ref[...] = wo_ref[...].astype(wo16_ref.dtype)


def conv_module(plain, conv_w, conv_b, cnorm_g, cnorm_b, w_ffn_out, layer):
    s = plain.shape[0]
    steps = s // CONV_TILE
    nh = CONV_TILE // CONV_HALO
    width = 2 * CONV_CH
    _, d_ff, d_out = w_ffn_out.shape
    wo_rows = d_ff // steps
    assert wo_rows * steps == d_ff and wo_rows % BF16_ROWS == 0
    vec_spec = pl.BlockSpec((None, 1, CONV_CH), lambda i: (layer, 0, 0))
    row = lambda v: v.reshape(v.shape[0], 1, CONV_CH)
    return pl.pallas_call(
        _conv_kernel,
        grid=(steps,),
        in_specs=[pl.BlockSpec((CONV_HALO, width), lambda i: (jnp.maximum(i * nh - 1, 0), 0)),
                  pl.BlockSpec((CONV_TILE, width), lambda i: (i, 0)),
                  pl.BlockSpec((CONV_HALO, width),
                               lambda i: (jnp.minimum((i + 1) * nh, s // CONV_HALO - 1), 0)),
                  pl.BlockSpec((None, CONV_WIDTH, CONV_CH), lambda i: (layer, 0, 0)),
                  vec_spec, vec_spec, vec_spec,
                  pl.BlockSpec((None, wo_rows, d_out), lambda i: (layer, i, 0))],
        out_specs=[pl.BlockSpec((CONV_TILE, CONV_CH), lambda i: (i, 0)),
                   pl.BlockSpec((wo_rows, d_out), lambda i: (i, 0))],
        out_shape=[jax.ShapeDtypeStruct((s, CONV_CH), BF16),
                   jax.ShapeDtypeStruct((d_ff, d_out), BF16)],
        scratch_shapes=[pltpu.VMEM((CONV_TILE + 2 * CONV_HALO, CONV_CH), F32),
                        pltpu.VMEM((SUBLANES - 1, CONV_TILE + 2 * CONV_HALO - SUBLANES, CONV_CH), F32)],
        compiler_params=_params(1),
        name="conv_module",
    )(plain, plain, plain, conv_w, row(conv_b), row(cnorm_g), row(cnorm_b), w_ffn_out)


def _band_tables(bq, half, step):
    shape = (bq, bq + 2 * half)
    qi = lax.broadcasted_iota(jnp.int32, shape, 0)
    kj = lax.broadcasted_iota(jnp.int32, shape, 1)
    rel = jnp.abs(kj - half - qi)
    return rel <= half, (step * rel).astype(F32), kj


def _edge_valid(band, kj, bq, half, blk, n_blocks, first, last):
    valid = band
    if first:
        valid = valid & (kj >= jnp.where(blk > 0, 0, half))
    if last:
        valid = valid & (kj < jnp.where(blk < n_blocks - 1, bq + 2 * half, bq + half))
    return valid


def _scores(q, k):
    return lax.dot_general(q, k, (((1,), (1,)), ((), ())), preferred_element_type=F32)


def _softmax_rows(sc, bias2, valid, sink2):
    t = jnp.where(valid, sc * (HEAD_DIM ** -0.5 * LOG2E) + bias2, NEG * LOG2E)
    m2 = jnp.max(t, axis=-1, keepdims=True)
    if sink2 is not None:
        m2 = jnp.maximum(m2, sink2)
    p = jnp.exp2(t - m2)
    denom = jnp.sum(p, axis=-1, keepdims=True)
    if sink2 is not None:
        denom = denom + jnp.exp2(sink2 - m2)
    return p, m2, denom


def _swa_kernel(slopes_ref, sink_ref, q_ref, kp_ref, kc_ref, kn_ref, vp_ref, vc_ref, vn_ref, wg_ref,
                o_ref, wg16_ref, *, tq, half, n_blocks, layer):
    i = pl.program_id(0)
    hk = pl.program_id(1)

    @pl.when(i * pl.num_programs(1) + hk < N_GATE_BLOCKS)
    def _():
        _cast_rows(wg_ref, wg16_ref)

    bq = half
    nsub = tq // bq
    kcat = jnp.concatenate([kp_ref[...], kc_ref[...], kn_ref[...]], axis=0)
    vcat = jnp.concatenate([vp_ref[...], vc_ref[...], vn_ref[...]], axis=0)
    band, dist, kj = _band_tables(bq, half, 1)
    heads = [hk * B_GROUP + g for g in range(B_GROUP)]
    biases = [(-LOG2E * slopes_ref[hd]) * dist for hd in heads]
    sinks = [LOG2E * sink_ref[layer, hd] for hd in heads]
    for s in range(nsub):
        rows = slice(s * bq, (s + 1) * bq)
        win = slice(s * bq, (s + 1) * bq + 2 * half)
        valid = _edge_valid(band, kj, bq, half, i * nsub + s, n_blocks, s == 0, s == nsub - 1)
        q_all = jnp.concatenate(
            [q_ref[rows, g * HEAD_DIM:(g + 1) * HEAD_DIM] for g in range(B_GROUP)], axis=0)
        sc_all = _scores(q_all, kcat[win])
        ps, denoms = [], []
        for g in range(B_GROUP):
            p, _, denom = _softmax_rows(sc_all[g * bq:(g + 1) * bq], biases[g], valid, sinks[g])
            ps.append(p.astype(BF16))
            denoms.append(denom)
        o_all = jnp.dot(jnp.concatenate(ps, axis=0), vcat[win], preferred_element_type=F32)
        for g in range(B_GROUP):
            o = o_all[g * bq:(g + 1) * bq] / denoms[g]
            o_ref[rows, g * HEAD_DIM:(g + 1) * HEAD_DIM] = o.astype(o_ref.dtype)


def windowed_gqa(plain, slopes, sink, w_in, layer, tq=1024):
    s = plain.shape[0]
    d = w_in.shape[1]
    assert (s // tq) * B_KV_HEADS >= N_GATE_BLOCKS
    gate_blk = lambda i, h: jnp.minimum(i * B_KV_HEADS + h, N_GATE_BLOCKS - 1)
    half = B_HALF_WINDOW
    nh = tq // half
    qw = B_GROUP * HEAD_DIM
    q_blk0 = 2 * CONV_CH // qw
    k_blk0 = (2 * CONV_CH + B_Q_HEADS * HEAD_DIM) // HEAD_DIM
    v_blk0 = k_blk0 + B_KV_HEADS

    def halo_specs(blk0):
        return [pl.BlockSpec((half, HEAD_DIM), lambda i, h: (jnp.maximum(i * nh - 1, 0), blk0 + h)),
                pl.BlockSpec((tq, HEAD_DIM), lambda i, h: (i, blk0 + h)),
                pl.BlockSpec((half, HEAD_DIM),
                             lambda i, h: (jnp.minimum((i + 1) * nh, s // half - 1), blk0 + h))]

    smem = pl.BlockSpec(memory_space=pltpu.SMEM)
    return pl.pallas_call(
        functools.partial(_swa_kernel, tq=tq, half=half, n_blocks=s // half, layer=layer),
        grid=(s // tq, B_KV_HEADS),
        in_specs=[smem, smem, pl.BlockSpec((tq, qw), lambda i, h: (i, q_blk0 + h))]
        + halo_specs(k_blk0) + halo_specs(v_blk0)
        + [pl.BlockSpec((None, d, COL), lambda i, h: (layer, 0, GATE_BLOCK0 + gate_blk(i, h)))],
        out_specs=[pl.BlockSpec((tq, qw), lambda i, h: (i, h)),
                   pl.BlockSpec((d, COL), lambda i, h: (0, gate_blk(i, h)))],
        out_shape=[jax.ShapeDtypeStruct((s, B_Q_HEADS * HEAD_DIM), BF16),
                   jax.ShapeDtypeStruct((d, N_GATE_BLOCKS * COL), BF16)],
        compiler_params=_params(2),
        name="windowed_gqa",
    )(slopes, sink, plain, plain, plain, plain, plain, plain, plain, w_in)


def _dilated_kernel(slopes_ref, *refs, tokens, bq, half, seq_len):
    o_ref, u_scr, m_scr, d_scr = refs[-4:]
    i = pl.program_id(0)
    h = pl.program_id(1)
    for gi, (_, r) in enumerate(C_PATTERNS):
        q_ref, kp_ref, kc_ref, kn_ref, vp_ref, vc_ref, vn_ref = refs[7 * gi:7 * gi + 7]
        rows = tokens // r
        nsub = rows // bq
        n_blocks = seq_len // r // bq
        band, dist, kj = _band_tables(bq, half, r)
        bias = (-LOG2E * slopes_ref[B_Q_HEADS + gi * C_HEADS_PER_GROUP + h]) * dist
        for c in range(r):
            kcat = jnp.concatenate([kp_ref[c], kc_ref[c], kn_ref[c]], axis=0)
            vcat = jnp.concatenate([vp_ref[c], vc_ref[c], vn_ref[c]], axis=0)
            for s in range(nsub):
                win = slice(s * bq, (s + 1) * bq + 2 * half)
                valid = _edge_valid(band, kj, bq, half, i * nsub + s, n_blocks, s == 0, s == nsub - 1)
                p, m2, denom = _softmax_rows(_scores(q_ref[c, s * bq:(s + 1) * bq, :], kcat[win]),
                                             bias, valid, None)
                dst = pl.ds(s * bq * r + c, bq, stride=r) if r > 1 else pl.ds(s * bq, bq)
                u_scr[gi, dst, :] = jnp.dot(p.astype(BF16), vcat[win], preferred_element_type=F32)
                m_scr[gi, dst, :] = jnp.broadcast_to(m2, (bq, HEAD_DIM))
                d_scr[gi, dst, :] = jnp.broadcast_to(denom, (bq, HEAD_DIM))
    m_max = jnp.maximum(jnp.maximum(m_scr[0], m_scr[1]), m_scr[2])
    num = den = None
    for gi in range(len(C_PATTERNS)):
        w = jnp.exp2(m_scr[gi] - m_max)
        num = w * u_scr[gi] if num is None else num + w * u_scr[gi]
        den = w * d_scr[gi] if den is None else den + w * d_scr[gi]
    o_ref[...] = (num / den).astype(o_ref.dtype)


def dilated_attention(lattices, slopes, seq_len, tokens=C_TOKENS, bq=C_BLOCK_Q, half=C_HALF):
    nq = C_HEADS_PER_GROUP
    in_specs = [pl.BlockSpec(memory_space=pltpu.SMEM)]
    args = [slopes]
    for (_, r), lat in zip(C_PATTERNS, lattices):
        rows = tokens // r
        nh = rows // half
        last = seq_len // r // half - 1

        def specs(blk0, r=r, rows=rows, nh=nh, last=last):
            return [pl.BlockSpec((r, half, HEAD_DIM), lambda i, h: (0, jnp.maximum(i * nh - 1, 0), blk0 + h)),
                    pl.BlockSpec((r, rows, HEAD_DIM), lambda i, h: (0, i, blk0 + h)),
                    pl.BlockSpec((r, half, HEAD_DIM), lambda i, h: (0, jnp.minimum((i + 1) * nh, last), blk0 + h))]

        in_specs += [pl.BlockSpec((r, rows, HEAD_DIM), lambda i, h: (0, i, h))] + specs(nq) + specs(2 * nq)
        args += [lat] * 7
    return pl.pallas_call(
        functools.partial(_dilated_kernel, tokens=tokens, bq=bq, half=half, seq_len=seq_len),
        grid=(seq_len // tokens, nq),
        in_specs=in_specs,
        out_specs=pl.BlockSpec((tokens, HEAD_DIM), lambda i, h: (i, h)),
        out_shape=jax.ShapeDtypeStruct((seq_len, nq * HEAD_DIM), BF16),
        scratch_shapes=[pltpu.VMEM((len(C_PATTERNS), tokens, HEAD_DIM), F32)] * 3,
        compiler_params=_params(2),
        name="dilated_attention",
    )(*args)


def _alibi_slopes():
    h = jnp.arange(1, N_ATTN_HEADS + 1, dtype=F32)
    return jnp.exp2(-ALIBI_MAX_EXP * h / N_ATTN_HEADS)


def kernel(x, ln1_g, w_in, conv_w, conv_b, cnorm_g, cnorm_b, w_a, sink, w_b, w_c, w_o, ln2_g,
           w_ffn_in, w_ffn_out, lnf_g):
    b, s, d = x.shape
    depth = ln1_g.shape[0]
    assert depth >= 1
    slopes = _alibi_slopes()
    outs = []
    for xb in jnp.split(x, b, axis=0):
        xb = xb.reshape(s, d)
        for l in range(depth):
            lat0, h = norm_proj_undilated(xb, ln1_g[l], w_in, l)
            plain = proj_plain(h, w_in, l)
            lattices = [lat0] + [proj_lattice(h, w_in, l, gi, r)
                                 for gi, (_, r) in enumerate(C_PATTERNS) if gi > 0]
            ca, w_out16 = conv_module(plain, conv_w, conv_b, cnorm_g, cnorm_b, w_ffn_out, l)
            ob, gates16 = windowed_gqa(plain, slopes, sink, w_in, l)
            oc = dilated_attention(lattices, slopes, s)
            mixed = gated_merge(h, ca, ob, oc, gates16, w_a, w_b, w_c, l)
            xb, h2 = out_proj_norm(mixed, w_o, l, xb, ln2_g[l])
            act = swiglu_in(h2, w_ffn_in, l)
            if l < depth - 1:
                xb = matmul_residual(act, w_out16, xb)
            else:
                xb = matmul_residual_norm(act, w_out16, xb, lnf_g)
        outs.append(xb.reshape(1, s, d))
    return outs[0] if b == 1 else jnp.concatenate(outs, axis=0)
```

```python
import functools

import jax
import jax.numpy as jnp
from jax import lax
from jax.experimental import pallas as pl
from jax.experimental.pallas import tpu as pltpu

F32 = jnp.float32
BF16 = jnp.bfloat16

D_MODEL = 2048
HEAD_DIM = 128
CONV_CH = 512
CONV_WIDTH = 31
CONV_HALF = CONV_WIDTH // 2
B_Q_HEADS = 8
B_KV_HEADS = 2
B_GROUP = B_Q_HEADS // B_KV_HEADS
B_HALF_WINDOW = 128
C_PATTERNS = ((128, 1), (512, 4), (2048, 16))
C_HEADS_PER_GROUP = 4
C_HALF = 64
assert C_PATTERNS[0][1] == 1 and all(w // (2 * r) == C_HALF for w, r in C_PATTERNS)
N_BRANCHES = 3
N_ATTN_HEADS = B_Q_HEADS + len(C_PATTERNS) * C_HEADS_PER_GROUP
ALIBI_MAX_EXP = 8.0
D_FF = -(-8 * D_MODEL // (3 * 256)) * 256
EPS = 1e-6
NEG = -1e30
LOG2E = 1.4426950408889634

LANES = 128
MXU_COLS = 256
SUBLANES = 8
BF16_ROWS = 16
LATTICE_STRIDE = 4
COL = 512
PLAIN_COLS = 5 * COL
QC_BLOCK0 = 5
GATE_BLOCK0 = 14
N_GATE_BLOCKS = N_BRANCHES * D_MODEL // COL
CONV_TILE = 1024
CONV_HALO = 16
C_TOKENS = 2048
C_BLOCK_Q = 128

VMEM_LIMIT = 56 * 1024 * 1024


def _params(n_axes, flags=None):
    return pltpu.CompilerParams(
        dimension_semantics=("arbitrary",) * n_axes, vmem_limit_bytes=VMEM_LIMIT, flags=flags)


def _cast_rows(src_ref, dst_ref, chunk=256):
    rows = src_ref.shape[0]
    chunk = min(chunk, rows)

    def body(c, carry):
        sl = pl.ds(pl.multiple_of(c * chunk, chunk), chunk)
        dst_ref[sl, :] = src_ref[sl, :].astype(dst_ref.dtype)
        return carry

    lax.fori_loop(0, rows // chunk, body, 0)


def _rms(x, g):
    ms = jnp.mean(x * x, axis=-1, keepdims=True)
    return x * lax.rsqrt(ms + EPS) * g


def _ws_mm_kernel(a_ref, w_ref, o_ref, w16_ref):
    @pl.when(pl.program_id(1) == 0)
    def _():
        _cast_rows(w_ref, w16_ref)

    o_ref[...] = jnp.dot(a_ref[...], w16_ref[...],
                         preferred_element_type=F32).astype(o_ref.dtype)


def proj_plain(h, w_in, layer, tm=1024, tn=PLAIN_COLS // 2):
    s, d = h.shape
    return pl.pallas_call(
        _ws_mm_kernel,
        grid=(PLAIN_COLS // tn, s // tm),
        in_specs=[pl.BlockSpec((tm, d), lambda n, m: (m, 0)),
                  pl.BlockSpec((None, d, tn), lambda n, m: (layer, 0, n))],
        out_specs=pl.BlockSpec((tm, tn), lambda n, m: (m, n)),
        out_shape=jax.ShapeDtypeStruct((s, PLAIN_COLS), BF16),
        scratch_shapes=[pltpu.VMEM((d, tn), BF16)],
        compiler_params=_params(2),
        name="proj_plain",
    )(h, w_in)


def _lattice_kernel(a_ref, wq_ref, wk_ref, wv_ref, o_ref, w16_ref, acc_ref, tmp_ref, *, r):
    @pl.when(pl.program_id(0) == 0)
    def _():
        for j, w_ref in enumerate((wq_ref, wk_ref, wv_ref)):
            _cast_rows(w_ref, w16_ref.at[j])

    a = a_ref[...]
    rows = a.shape[0] // r
    slabs = COL // LANES
    for j in range(3):
        acc = jnp.dot(a, w16_ref[j], preferred_element_type=F32)
        for s in range(slabs):
            acc_ref[j * slabs + s] = acc[:, s * LANES:(s + 1) * LANES]
        for s in range(slabs):
            slab = acc_ref.at[j * slabs + s]
            col = j * COL + s * LANES
            if r == LATTICE_STRIDE:
                for c in range(r):
                    o_ref[c, :, col:col + LANES] = slab[pl.ds(c, rows, stride=r), :].astype(o_ref.dtype)
                continue
            assert r == LATTICE_STRIDE ** 2
            tmp = tmp_ref.at[j * slabs + s]
            for c0 in range(LATTICE_STRIDE):
                tmp[c0] = slab[pl.ds(c0, rows * LATTICE_STRIDE, stride=LATTICE_STRIDE), :]
            for c0 in range(LATTICE_STRIDE):
                for c1 in range(LATTICE_STRIDE):
                    o_ref[c0 + LATTICE_STRIDE * c1, :, col:col + LANES] = (
                        tmp[c0, pl.ds(c1, rows, stride=LATTICE_STRIDE), :].astype(o_ref.dtype))


def _norm_proj_kernel(x_ref, g_ref, wq_ref, wk_ref, wv_ref, o_ref, h_ref, w16_ref):
    @pl.when(pl.program_id(0) == 0)
    def _():
        for j, w_ref in enumerate((wq_ref, wk_ref, wv_ref)):
            _cast_rows(w_ref, w16_ref.at[j])

    half = x_ref.shape[0] // 2
    for rows in (slice(0, half), slice(half, 2 * half)):
        h = _rms(x_ref[rows, :], g_ref[...]).astype(h_ref.dtype)
        h_ref[rows, :] = h
        for j in range(3):
            o_ref[0, rows, j * COL:(j + 1) * COL] = jnp.dot(
                h, w16_ref[j], preferred_element_type=F32).astype(o_ref.dtype)


def norm_proj_undilated(x, g, w_in, layer, tm=1024):
    s, d = x.shape

    def w_spec(j):
        return pl.BlockSpec((None, d, COL), lambda m: (layer, 0, QC_BLOCK0 + 3 * j),
                            pipeline_mode=pl.Buffered(1))

    return pl.pallas_call(
        _norm_proj_kernel,
        grid=(s // tm,),
        in_specs=[pl.BlockSpec((tm, d), lambda m: (m, 0)), pl.BlockSpec((1, d), lambda m: (0, 0)),
                  w_spec(0), w_spec(1), w_spec(2)],
        out_specs=[pl.BlockSpec((1, tm, 3 * COL), lambda m: (0, m, 0)),
                   pl.BlockSpec((tm, d), lambda m: (m, 0))],
        out_shape=[jax.ShapeDtypeStruct((1, s, 3 * COL), BF16), jax.ShapeDtypeStruct((s, d), BF16)],
        scratch_shapes=[pltpu.VMEM((3, d, COL), BF16)],
        compiler_params=_params(1),
        name="norm_proj_undilated",
    )(x, g.reshape(1, d), w_in, w_in, w_in)


def proj_lattice(h, w_in, layer, group, r, tm=1024):
    s, d = h.shape
    assert r in (LATTICE_STRIDE, LATTICE_STRIDE ** 2)
    n_slabs = 3 * COL // LANES
    two_pass_rows = tm // LATTICE_STRIDE if r == LATTICE_STRIDE ** 2 else SUBLANES

    def w_spec(j):
        return pl.BlockSpec((None, d, COL), lambda m: (layer, 0, QC_BLOCK0 + group + 3 * j),
                            pipeline_mode=pl.Buffered(1))

    return pl.pallas_call(
        functools.partial(_lattice_kernel, r=r),
        grid=(s // tm,),
        in_specs=[pl.BlockSpec((tm, d), lambda m: (m, 0)), w_spec(0), w_spec(1), w_spec(2)],
        out_specs=pl.BlockSpec((r, tm // r, 3 * COL), lambda m: (0, m, 0)),
        out_shape=jax.ShapeDtypeStruct((r, s // r, 3 * COL), BF16),
        scratch_shapes=[pltpu.VMEM((3, d, COL), BF16),
                        pltpu.VMEM((n_slabs, tm, LANES), F32),
                        pltpu.VMEM((n_slabs, LATTICE_STRIDE, two_pass_rows, LANES), F32)],
        compiler_params=_params(1),
        name=f"proj_lattice_r{r}",
    )(h, w_in, w_in, w_in)


def _merge_kernel(h_ref, ca_ref, ob_ref, oc_ref, wg0_ref, wg1_ref, wg2_ref,
                  wa_ref, wb_ref, wc_ref, o_ref, wa16_ref, wb16_ref, wc16_ref):
    @pl.when(pl.program_id(1) == 0)
    def _():
        _cast_rows(wa_ref, wa16_ref)
        _cast_rows(wb_ref, wb16_ref)
        _cast_rows(wc_ref, wc16_ref)

    h = h_ref[...]
    ys = (ca_ref[...], ob_ref[...], oc_ref[...])
    ws = (wa16_ref, wb16_ref, wc16_ref)
    wgs = (wg0_ref, wg1_ref, wg2_ref)
    for c0 in range(0, o_ref.shape[1], MXU_COLS):
        cols = slice(c0, c0 + MXU_COLS)
        mixed = None
        for b in range(N_BRANCHES):
            gate = jax.nn.sigmoid(jnp.dot(h, wgs[b][:, cols], preferred_element_type=F32))
            term = gate * jnp.dot(ys[b], ws[b][:, cols], preferred_element_type=F32)
            mixed = term if mixed is None else mixed + term
        o_ref[:, cols] = mixed.astype(o_ref.dtype)


def gated_merge(h, ca, ob, oc, gates16, w_a, w_b, w_c, layer, tm=1024, tn=COL):
    s, d = h.shape
    nj = d // tn
    widths = (ca.shape[1], ob.shape[1], oc.shape[1])

    def gate_spec(b):
        return pl.BlockSpec((d, tn), lambda n, m: (0, b * nj + n))

    def lhs_spec(width):
        return pl.BlockSpec((tm, width), lambda n, m: (m, 0))

    def w_spec(width):
        return pl.BlockSpec((None, width, tn), lambda n, m: (layer, 0, n),
                            pipeline_mode=pl.Buffered(1))

    return pl.pallas_call(
        _merge_kernel,
        grid=(nj, s // tm),
        in_specs=[lhs_spec(d)] + [lhs_spec(w) for w in widths]
        + [gate_spec(0), gate_spec(1), gate_spec(2)] + [w_spec(w) for w in widths],
        out_specs=pl.BlockSpec((tm, tn), lambda n, m: (m, n)),
        out_shape=jax.ShapeDtypeStruct((s, d), BF16),
        scratch_shapes=[pltpu.VMEM((w, tn), BF16) for w in widths],
        compiler_params=_params(2),
        name="gated_merge",
    )(h, ca, ob, oc, gates16, gates16, gates16, w_a, w_b, w_c)


def _out_proj_kernel(a_ref, w_ref, x_ref, g_ref, xo_ref, ho_ref, w16_ref):
    @pl.when(pl.program_id(0) == 0)
    def _():
        _cast_rows(w_ref, w16_ref)

    half = a_ref.shape[0] // 2
    for rows in (slice(0, half), slice(half, 2 * half)):
        x = x_ref[rows, :] + jnp.dot(a_ref[rows, :], w16_ref[...], preferred_element_type=F32)
        xo_ref[rows, :] = x
        ho_ref[rows, :] = _rms(x, g_ref[...]).astype(ho_ref.dtype)


def out_proj_norm(a, w_o, layer, x, g, tm=512):
    s, k = a.shape
    d = x.shape[1]
    row = pl.BlockSpec((tm, d), lambda m: (m, 0))
    return pl.pallas_call(
        _out_proj_kernel,
        grid=(s // tm,),
        in_specs=[pl.BlockSpec((tm, k), lambda m: (m, 0)),
                  pl.BlockSpec((None, k, d), lambda m: (layer, 0, 0), pipeline_mode=pl.Buffered(1)),
                  row,
                  pl.BlockSpec((1, d), lambda m: (0, 0))],
        out_specs=[row, row],
        out_shape=[jax.ShapeDtypeStruct((s, d), F32), jax.ShapeDtypeStruct((s, d), BF16)],
        scratch_shapes=[pltpu.VMEM((k, d), BF16)],
        compiler_params=_params(1),
        name="out_proj_norm",
    )(a, w_o, x, g.reshape(1, d))


def _swiglu_kernel(h_ref, wg_ref, wu_ref, o_ref, w16_ref):
    @pl.when(pl.program_id(1) == 0)
    def _():
        _cast_rows(wg_ref, w16_ref.at[0])
        _cast_rows(wu_ref, w16_ref.at[1])

    h = h_ref[...]
    for c0 in range(0, o_ref.shape[1], MXU_COLS):
        cols = slice(c0, c0 + MXU_COLS)
        g = jnp.dot(h, w16_ref[0, :, cols], preferred_element_type=F32)
        u = jnp.dot(h, w16_ref[1, :, cols], preferred_element_type=F32)
        o_ref[:, cols] = (jax.nn.silu(g) * u).astype(o_ref.dtype)


def swiglu_in(h, w_ffn_in, layer, tm=2048, tn=COL):
    s, d = h.shape
    nj = D_FF // tn
    return pl.pallas_call(
        _swiglu_kernel,
        grid=(nj, s // tm),
        in_specs=[pl.BlockSpec((tm, d), lambda n, m: (m, 0)),
                  pl.BlockSpec((None, d, tn), lambda n, m: (layer, 0, n)),
                  pl.BlockSpec((None, d, tn), lambda n, m: (layer, 0, nj + n))],
        out_specs=pl.BlockSpec((tm, tn), lambda n, m: (m, n)),
        out_shape=jax.ShapeDtypeStruct((s, D_FF), BF16),
        scratch_shapes=[pltpu.VMEM((2, d, tn), BF16)],
        compiler_params=_params(2),
        name="swiglu_in",
    )(h, w_ffn_in, w_ffn_in)


def _mm_residual_kernel(a_ref, w_ref, x_ref, o_ref):
    o_ref[...] = x_ref[...] + jnp.dot(a_ref[...], w_ref[...], preferred_element_type=F32)


def matmul_residual(a, w16, x, tm=1024, tn=COL):
    s, k = a.shape
    d = w16.shape[1]
    return pl.pallas_call(
        _mm_residual_kernel,
        grid=(s // tm, d // tn),
        in_specs=[pl.BlockSpec((tm, k), lambda i, j: (i, 0)),
                  pl.BlockSpec((k, tn), lambda i, j: (0, j)),
                  pl.BlockSpec((tm, tn), lambda i, j: (i, j))],
        out_specs=pl.BlockSpec((tm, tn), lambda i, j: (i, j)),
        out_shape=jax.ShapeDtypeStruct((s, d), F32),
        compiler_params=_params(2),
        name="matmul_residual",
    )(a, w16, x)


def _mm_residual_norm_kernel(a_ref, w_ref, x_ref, g_ref, o_ref, ss_ref):
    n = pl.program_id(1)
    tn = x_ref.shape[1]
    x = x_ref[...] + jnp.dot(a_ref[...], w_ref[...], preferred_element_type=F32)
    ss = jnp.sum(x * x, axis=-1, keepdims=True)

    @pl.when(n == 0)
    def _():
        ss_ref[...] = jnp.broadcast_to(ss, ss_ref.shape)

    @pl.when(n > 0)
    def _():
        ss_ref[...] += ss

    for j in range(o_ref.shape[1] // tn):
        @pl.when(n == j)
        def _():
            o_ref[:, j * tn:(j + 1) * tn] = x

    @pl.when(n == pl.num_programs(1) - 1)
    def _():
        ms = ss_ref[:, 0:1] * (1.0 / o_ref.shape[1])
        o_ref[...] = o_ref[...] * lax.rsqrt(ms + EPS) * g_ref[...]


def matmul_residual_norm(a, w16, x, g, tm=1024, tn=COL):
    s, k = a.shape
    d = w16.shape[1]
    return pl.pallas_call(
        _mm_residual_norm_kernel,
        grid=(s // tm, d // tn),
        in_specs=[pl.BlockSpec((tm, k), lambda i, j: (i, 0)),
                  pl.BlockSpec((k, tn), lambda i, j: (0, j)),
                  pl.BlockSpec((tm, tn), lambda i, j: (i, j)),
                  pl.BlockSpec((1, d), lambda i, j: (0, 0))],
        out_specs=pl.BlockSpec((tm, d), lambda i, j: (i, 0)),
        out_shape=jax.ShapeDtypeStruct((s, d), F32),
        scratch_shapes=[pltpu.VMEM((tm, LANES), F32)],
        compiler_params=_params(2),
        name="matmul_residual_norm",
    )(a, w16, x, g.reshape(1, d))


def _conv_tile(up_ref, uc_ref, un_ref, w_ref, b_ref, g_ref, beta_ref, o_ref, z_ref, sh_ref,
               *, first, last, chunk=64):
    tm, halo = CONV_TILE, CONV_HALO

    def glu(u):
        u = u.astype(F32)
        return u[:, :CONV_CH] * jax.nn.sigmoid(u[:, CONV_CH:])

    z_ref[0:halo] = jnp.where(first, 0.0, glu(up_ref[...]))
    z_ref[halo:halo + tm] = glu(uc_ref[...])
    z_ref[halo + tm:2 * halo + tm] = jnp.where(last, 0.0, glu(un_ref[...]))
    span = sh_ref.shape[1]
    for j in range(1, SUBLANES):
        sh_ref[j - 1] = z_ref[j:j + span, :]

    for rc in range(tm // chunk):
        acc = jnp.zeros((chunk, CONV_CH), F32)
        for k in range(CONV_WIDTH):
            off = rc * chunk + halo - CONV_HALF + k
            j = off % SUBLANES
            src = z_ref if j == 0 else sh_ref.at[j - 1]
            acc = acc + src[off - j:off - j + chunk, :] * w_ref[k:k + 1, :]
        z = acc + b_ref[...]
        mu = jnp.mean(z, axis=-1, keepdims=True)
        zc = z - mu
        var = jnp.mean(zc * zc, axis=-1, keepdims=True)
        y = zc * lax.rsqrt(var + EPS) * g_ref[...] + beta_ref[...]
        o_ref[rc * chunk:(rc + 1) * chunk, :] = jax.nn.silu(y).astype(o_ref.dtype)


def _conv_kernel(up_ref, uc_ref, un_ref, w_ref, b_ref, g_ref, beta_ref, wo_ref,
                 o_ref, wo16_ref, z_ref, sh_ref):
    i = pl.program_id(0)
    _conv_tile(up_ref, uc_ref, un_ref, w_ref, b_ref, g_ref, beta_ref, o_ref, z_ref, sh_ref,
               first=i == 0, last=i == pl.num_programs(0) - 1)
    wo16_ref[...] = wo_ref[...].astype(wo16_ref.dtype)


def conv_module(plain, conv_w, conv_b, cnorm_g, cnorm_b, w_ffn_out, layer):
    s = plain.shape[0]
    steps = s // CONV_TILE
    nh = CONV_TILE // CONV_HALO
    width = 2 * CONV_CH
    _, d_ff, d_out = w_ffn_out.shape
    wo_rows = d_ff // steps
    assert wo_rows * steps == d_ff and wo_rows % BF16_ROWS == 0
    vec_spec = pl.BlockSpec((None, 1, CONV_CH), lambda i: (layer, 0, 0))
    row = lambda v: v.reshape(v.shape[0], 1, CONV_CH)
    return pl.pallas_call(
        _conv_kernel,
        grid=(steps,),
        in_specs=[pl.BlockSpec((CONV_HALO, width), lambda i: (jnp.maximum(i * nh - 1, 0), 0)),
                  pl.BlockSpec((CONV_TILE, width), lambda i: (i, 0)),
                  pl.BlockSpec((CONV_HALO, width),
                               lambda i: (jnp.minimum((i + 1) * nh, s // CONV_HALO - 1), 0)),
                  pl.BlockSpec((None, CONV_WIDTH, CONV_CH), lambda i: (layer, 0, 0)),
                  vec_spec, vec_spec, vec_spec,
                  pl.BlockSpec((None, wo_rows, d_out), lambda i: (layer, i, 0))],
        out_specs=[pl.BlockSpec((CONV_TILE, CONV_CH), lambda i: (i, 0)),
                   pl.BlockSpec((wo_rows, d_out), lambda i: (i, 0))],
        out_shape=[jax.ShapeDtypeStruct((s, CONV_CH), BF16),
                   jax.ShapeDtypeStruct((d_ff, d_out), BF16)],
        scratch_shapes=[pltpu.VMEM((CONV_TILE + 2 * CONV_HALO, CONV_CH), F32),
                        pltpu.VMEM((SUBLANES - 1, CONV_TILE + 2 * CONV_HALO - SUBLANES, CONV_CH), F32)],
        compiler_params=_params(1),
        name="conv_module",
    )(plain, plain, plain, conv_w, row(conv_b), row(cnorm_g), row(cnorm_b), w_ffn_out)


def _band_tables(bq, half, step):
    shape = (bq, bq + 2 * half)
    qi = lax.broadcasted_iota(jnp.int32, shape, 0)
    kj = lax.broadcasted_iota(jnp.int32, shape, 1)
    rel = jnp.abs(kj - half - qi)
    return rel <= half, (step * rel).astype(F32), kj


def _edge_valid(band, kj, bq, half, blk, n_blocks, first, last):
    valid = band
    if first:
        valid = valid & (kj >= jnp.where(blk > 0, 0, half))
    if last:
        valid = valid & (kj < jnp.where(blk < n_blocks - 1, bq + 2 * half, bq + half))
    return valid


def _scores(q, k):
    return lax.dot_general(q, k, (((1,), (1,)), ((), ())), preferred_element_type=F32)


def _softmax_rows(sc, bias2, valid, sink2):
    t = jnp.where(valid, sc * (HEAD_DIM ** -0.5 * LOG2E) + bias2, NEG * LOG2E)
    m2 = jnp.max(t, axis=-1, keepdims=True)
    if sink2 is not None:
        m2 = jnp.maximum(m2, sink2)
    p = jnp.exp2(t - m2)
    denom = jnp.sum(p, axis=-1, keepdims=True)
    if sink2 is not None:
        denom = denom + jnp.exp2(sink2 - m2)
    return p, m2, denom


def _swa_kernel(slopes_ref, sink_ref, q_ref, kp_ref, kc_ref, kn_ref, vp_ref, vc_ref, vn_ref, wg_ref,
                o_ref, wg16_ref, *, tq, half, n_blocks, layer):
    i = pl.program_id(0)
    hk = pl.program_id(1)

    @pl.when(i * pl.num_programs(1) + hk < N_GATE_BLOCKS)
    def _():
        _cast_rows(wg_ref, wg16_ref)

    bq = half
    nsub = tq // bq
    kcat = jnp.concatenate([kp_ref[...], kc_ref[...], kn_ref[...]], axis=0)
    vcat = jnp.concatenate([vp_ref[...], vc_ref[...], vn_ref[...]], axis=0)
    band, dist, kj = _band_tables(bq, half, 1)
    heads = [hk * B_GROUP + g for g in range(B_GROUP)]
    biases = [(-LOG2E * slopes_ref[hd]) * dist for hd in heads]
    sinks = [LOG2E * sink_ref[layer, hd] for hd in heads]
    for s in range(nsub):
        rows = slice(s * bq, (s + 1) * bq)
        win = slice(s * bq, (s + 1) * bq + 2 * half)
        valid = _edge_valid(band, kj, bq, half, i * nsub + s, n_blocks, s == 0, s == nsub - 1)
        q_all = jnp.concatenate(
            [q_ref[rows, g * HEAD_DIM:(g + 1) * HEAD_DIM] for g in range(B_GROUP)], axis=0)
        sc_all = _scores(q_all, kcat[win])
        ps, denoms = [], []
        for g in range(B_GROUP):
            p, _, denom = _softmax_rows(sc_all[g * bq:(g + 1) * bq], biases[g], valid, sinks[g])
            ps.append(p.astype(BF16))
            denoms.append(denom)
        o_all = jnp.dot(jnp.concatenate(ps, axis=0), vcat[win], preferred_element_type=F32)
        for g in range(B_GROUP):
            o = o_all[g * bq:(g + 1) * bq] / denoms[g]
            o_ref[rows, g * HEAD_DIM:(g + 1) * HEAD_DIM] = o.astype(o_ref.dtype)


def windowed_gqa(plain, slopes, sink, w_in, layer, tq=2048):
    s = plain.shape[0]
    d = w_in.shape[1]
    assert (s // tq) * B_KV_HEADS >= N_GATE_BLOCKS
    gate_blk = lambda i, h: jnp.minimum(i * B_KV_HEADS + h, N_GATE_BLOCKS - 1)
    half = B_HALF_WINDOW
    nh = tq // half
    qw = B_GROUP * HEAD_DIM
    q_blk0 = 2 * CONV_CH // qw
    k_blk0 = (2 * CONV_CH + B_Q_HEADS * HEAD_DIM) // HEAD_DIM
    v_blk0 = k_blk0 + B_KV_HEADS

    def halo_specs(blk0):
        return [pl.BlockSpec((half, HEAD_DIM), lambda i, h: (jnp.maximum(i * nh - 1, 0), blk0 + h)),
                pl.BlockSpec((tq, HEAD_DIM), lambda i, h: (i, blk0 + h)),
                pl.BlockSpec((half, HEAD_DIM),
                             lambda i, h: (jnp.minimum((i + 1) * nh, s // half - 1), blk0 + h))]

    smem = pl.BlockSpec(memory_space=pltpu.SMEM)
    return pl.pallas_call(
        functools.partial(_swa_kernel, tq=tq, half=half, n_blocks=s // half, layer=layer),
        grid=(s // tq, B_KV_HEADS),
        in_specs=[smem, smem, pl.BlockSpec((tq, qw), lambda i, h: (i, q_blk0 + h))]
        + halo_specs(k_blk0) + halo_specs(v_blk0)
        + [pl.BlockSpec((None, d, COL), lambda i, h: (layer, 0, GATE_BLOCK0 + gate_blk(i, h)))],
        out_specs=[pl.BlockSpec((tq, qw), lambda i, h: (i, h)),
                   pl.BlockSpec((d, COL), lambda i, h: (0, gate_blk(i, h)))],
        out_shape=[jax.ShapeDtypeStruct((s, B_Q_HEADS * HEAD_DIM), BF16),
                   jax.ShapeDtypeStruct((d, N_GATE_BLOCKS * COL), BF16)],
        compiler_params=_params(2),
        name="windowed_gqa",
    )(slopes, sink, plain, plain, plain, plain, plain, plain, plain, w_in)


def _dilated_kernel(slopes_ref, *refs, tokens, bq, half, seq_len):
    o_ref, u_scr, m_scr, d_scr = refs[-4:]
    i = pl.program_id(0)
    h = pl.program_id(1)
    for gi, (_, r) in enumerate(C_PATTERNS):
        q_ref, kp_ref, kc_ref, kn_ref, vp_ref, vc_ref, vn_ref = refs[7 * gi:7 * gi + 7]
        rows = tokens // r
        nsub = rows // bq
        n_blocks = seq_len // r // bq
        band, dist, kj = _band_tables(bq, half, r)
        bias = (-LOG2E * slopes_ref[B_Q_HEADS + gi * C_HEADS_PER_GROUP + h]) * dist
        for c in range(r):
            kcat = jnp.concatenate([kp_ref[c], kc_ref[c], kn_ref[c]], axis=0)
            vcat = jnp.concatenate([vp_ref[c], vc_ref[c], vn_ref[c]], axis=0)
            for s in range(nsub):
                win = slice(s * bq, (s + 1) * bq + 2 * half)
                valid = _edge_valid(band, kj, bq, half, i * nsub + s, n_blocks, s == 0, s == nsub - 1)
                p, m2, denom = _softmax_rows(_scores(q_ref[c, s * bq:(s + 1) * bq, :], kcat[win]),
                                             bias, valid, None)
                dst = pl.ds(s * bq * r + c, bq, stride=r) if r > 1 else pl.ds(s * bq, bq)
                u_scr[gi, dst, :] = jnp.dot(p.astype(BF16), vcat[win], preferred_element_type=F32)
                m_scr[gi, dst, :] = jnp.broadcast_to(m2, (bq, HEAD_DIM))
                d_scr[gi, dst, :] = jnp.broadcast_to(denom, (bq, HEAD_DIM))
    m_max = jnp.maximum(jnp.maximum(m_scr[0], m_scr[1]), m_scr[2])
    num = den = None
    for gi in range(len(C_PATTERNS)):
        w = jnp.exp2(m_scr[gi] - m_max)
        num = w * u_scr[gi] if num is None else num + w * u_scr[gi]
        den = w * d_scr[gi] if den is None else den + w * d_scr[gi]
    o_ref[...] = (num / den).astype(o_ref.dtype)


def dilated_attention(lattices, slopes, seq_len, tokens=C_TOKENS, bq=C_BLOCK_Q, half=C_HALF):
    nq = C_HEADS_PER_GROUP
    in_specs = [pl.BlockSpec(memory_space=pltpu.SMEM)]
    args = [slopes]
    for (_, r), lat in zip(C_PATTERNS, lattices):
        rows = tokens // r
        nh = rows // half
        last = seq_len // r // half - 1

        def specs(blk0, r=r, rows=rows, nh=nh, last=last):
            return [pl.BlockSpec((r, half, HEAD_DIM), lambda i, h: (0, jnp.maximum(i * nh - 1, 0), blk0 + h)),
                    pl.BlockSpec((r, rows, HEAD_DIM), lambda i, h: (0, i, blk0 + h)),
                    pl.BlockSpec((r, half, HEAD_DIM), lambda i, h: (0, jnp.minimum((i + 1) * nh, last), blk0 + h))]

        in_specs += [pl.BlockSpec((r, rows, HEAD_DIM), lambda i, h: (0, i, h))] + specs(nq) + specs(2 * nq)
        args += [lat] * 7
    return pl.pallas_call(
        functools.partial(_dilated_kernel, tokens=tokens, bq=bq, half=half, seq_len=seq_len),
        grid=(seq_len // tokens, nq),
        in_specs=in_specs,
        out_specs=pl.BlockSpec((tokens, HEAD_DIM), lambda i, h: (i, h)),
        out_shape=jax.ShapeDtypeStruct((seq_len, nq * HEAD_DIM), BF16),
        scratch_shapes=[pltpu.VMEM((len(C_PATTERNS), tokens, HEAD_DIM), F32)] * 3,
        compiler_params=_params(2),
        name="dilated_attention",
    )(*args)


def _alibi_slopes():
    h = jnp.arange(1, N_ATTN_HEADS + 1, dtype=F32)
    return jnp.exp2(-ALIBI_MAX_EXP * h / N_ATTN_HEADS)


def kernel(x, ln1_g, w_in, conv_w, conv_b, cnorm_g, cnorm_b, w_a, sink, w_b, w_c, w_o, ln2_g,
           w_ffn_in, w_ffn_out, lnf_g):
    b, s, d = x.shape
    depth = ln1_g.shape[0]
    assert depth >= 1
    slopes = _alibi_slopes()
    outs = []
    for xb in jnp.split(x, b, axis=0):
        xb = xb.reshape(s, d)
        for l in range(depth):
            lat0, h = norm_proj_undilated(xb, ln1_g[l], w_in, l)
            plain = proj_plain(h, w_in, l)
            lattices = [lat0] + [proj_lattice(h, w_in, l, gi, r)
                                 for gi, (_, r) in enumerate(C_PATTERNS) if gi > 0]
            ca, w_out16 = conv_module(plain, conv_w, conv_b, cnorm_g, cnorm_b, w_ffn_out, l)
            ob, gates16 = windowed_gqa(plain, slopes, sink, w_in, l)
            oc = dilated_attention(lattices, slopes, s)
            mixed = gated_merge(h, ca, ob, oc, gates16, w_a, w_b, w_c, l)
            xb, h2 = out_proj_norm(mixed, w_o, l, xb, ln2_g[l])
            act = swiglu_in(h2, w_ffn_in, l)
            if l < depth - 1:
                xb = matmul_residual(act, w_out16, xb)
            else:
                xb = matmul_residual_norm(act, w_out16, xb, lnf_g)
        outs.append(xb.reshape(1, s, d))
    return outs[0] if b == 1 else jnp.concatenate(outs, axis=0)
```

```python
import functools

import jax
import jax.numpy as jnp
from jax import lax
from jax.experimental import pallas as pl
from jax.experimental.pallas import tpu as pltpu

F32 = jnp.float32
BF16 = jnp.bfloat16

D_MODEL = 2048
HEAD_DIM = 128
CONV_CH = 512
CONV_WIDTH = 31
CONV_HALF = CONV_WIDTH // 2
B_Q_HEADS = 8
B_KV_HEADS = 2
B_GROUP = B_Q_HEADS // B_KV_HEADS
B_HALF_WINDOW = 128
C_PATTERNS = ((128, 1), (512, 4), (2048, 16))
C_HEADS_PER_GROUP = 4
C_HALF = 64
assert C_PATTERNS[0][1] == 1 and all(w // (2 * r) == C_HALF for w, r in C_PATTERNS)
N_BRANCHES = 3
N_ATTN_HEADS = B_Q_HEADS + len(C_PATTERNS) * C_HEADS_PER_GROUP
ALIBI_MAX_EXP = 8.0
D_FF = -(-8 * D_MODEL // (3 * 256)) * 256
EPS = 1e-6
NEG = -1e30
LOG2E = 1.4426950408889634

LANES = 128
MXU_COLS = 256
SUBLANES = 8
BF16_ROWS = 16
LATTICE_STRIDE = 4
COL = 512
PLAIN_COLS = 5 * COL
QC_BLOCK0 = 5
GATE_BLOCK0 = 14
N_GATE_BLOCKS = N_BRANCHES * D_MODEL // COL
CONV_TILE = 1024
CONV_HALO = 16
C_TOKENS = 2048
C_BLOCK_Q = 128

VMEM_LIMIT = 56 * 1024 * 1024


def _params(n_axes, flags=None):
    return pltpu.CompilerParams(
        dimension_semantics=("arbitrary",) * n_axes, vmem_limit_bytes=VMEM_LIMIT, flags=flags)


def _cast_rows(src_ref, dst_ref, chunk=256):
    rows = src_ref.shape[0]
    chunk = min(chunk, rows)

    def body(c, carry):
        sl = pl.ds(pl.multiple_of(c * chunk, chunk), chunk)
        dst_ref[sl, :] = src_ref[sl, :].astype(dst_ref.dtype)
        return carry

    lax.fori_loop(0, rows // chunk, body, 0)


def _rms(x, g):
    ms = jnp.mean(x * x, axis=-1, keepdims=True)
    return x * lax.rsqrt(ms + EPS) * g


def _ws_mm_kernel(a_ref, w_ref, o_ref, w16_ref):
    @pl.when(pl.program_id(1) == 0)
    def _():
        _cast_rows(w_ref, w16_ref)

    o_ref[...] = jnp.dot(a_ref[...], w16_ref[...],
                         preferred_element_type=F32).astype(o_ref.dtype)


def proj_plain(h, w_in, layer, tm=1024, tn=PLAIN_COLS // 2):
    s, d = h.shape
    return pl.pallas_call(
        _ws_mm_kernel,
        grid=(PLAIN_COLS // tn, s // tm),
        in_specs=[pl.BlockSpec((tm, d), lambda n, m: (m, 0)),
                  pl.BlockSpec((None, d, tn), lambda n, m: (layer, 0, n))],
        out_specs=pl.BlockSpec((tm, tn), lambda n, m: (m, n)),
        out_shape=jax.ShapeDtypeStruct((s, PLAIN_COLS), BF16),
        scratch_shapes=[pltpu.VMEM((d, tn), BF16)],
        compiler_params=_params(2),
        name="proj_plain",
    )(h, w_in)


def _lattice_kernel(a_ref, wq_ref, wk_ref, wv_ref, o_ref, w16_ref, acc_ref, tmp_ref, *, r):
    @pl.when(pl.program_id(0) == 0)
    def _():
        for j, w_ref in enumerate((wq_ref, wk_ref, wv_ref)):
            _cast_rows(w_ref, w16_ref.at[j])

    a = a_ref[...]
    rows = a.shape[0] // r
    slabs = COL // LANES
    for j in range(3):
        acc = jnp.dot(a, w16_ref[j], preferred_element_type=F32)
        for s in range(slabs):
            acc_ref[j * slabs + s] = acc[:, s * LANES:(s + 1) * LANES]
        for s in range(slabs):
            slab = acc_ref.at[j * slabs + s]
            col = j * COL + s * LANES
            if r == LATTICE_STRIDE:
                for c in range(r):
                    o_ref[c, :, col:col + LANES] = slab[pl.ds(c, rows, stride=r), :].astype(o_ref.dtype)
                continue
            assert r == LATTICE_STRIDE ** 2
            tmp = tmp_ref.at[j * slabs + s]
            for c0 in range(LATTICE_STRIDE):
                tmp[c0] = slab[pl.ds(c0, rows * LATTICE_STRIDE, stride=LATTICE_STRIDE), :]
            for c0 in range(LATTICE_STRIDE):
                for c1 in range(LATTICE_STRIDE):
                    o_ref[c0 + LATTICE_STRIDE * c1, :, col:col + LANES] = (
                        tmp[c0, pl.ds(c1, rows, stride=LATTICE_STRIDE), :].astype(o_ref.dtype))


def _norm_proj_kernel(x_ref, g_ref, wq_ref, wk_ref, wv_ref, o_ref, h_ref, w16_ref):
    @pl.when(pl.program_id(0) == 0)
    def _():
        for j, w_ref in enumerate((wq_ref, wk_ref, wv_ref)):
            _cast_rows(w_ref, w16_ref.at[j])

    half = x_ref.shape[0] // 2
    for rows in (slice(0, half), slice(half, 2 * half)):
        h = _rms(x_ref[rows, :], g_ref[...]).astype(h_ref.dtype)
        h_ref[rows, :] = h
        for j in range(3):
            o_ref[0, rows, j * COL:(j + 1) * COL] = jnp.dot(
                h, w16_ref[j], preferred_element_type=F32).astype(o_ref.dtype)


def norm_proj_undilated(x, g, w_in, layer, tm=1024):
    s, d = x.shape

    def w_spec(j):
        return pl.BlockSpec((None, d, COL), lambda m: (layer, 0, QC_BLOCK0 + 3 * j),
                            pipeline_mode=pl.Buffered(1))

    return pl.pallas_call(
        _norm_proj_kernel,
        grid=(s // tm,),
        in_specs=[pl.BlockSpec((tm, d), lambda m: (m, 0)), pl.BlockSpec((1, d), lambda m: (0, 0)),
                  w_spec(0), w_spec(1), w_spec(2)],
        out_specs=[pl.BlockSpec((1, tm, 3 * COL), lambda m: (0, m, 0)),
                   pl.BlockSpec((tm, d), lambda m: (m, 0))],
        out_shape=[jax.ShapeDtypeStruct((1, s, 3 * COL), BF16), jax.ShapeDtypeStruct((s, d), BF16)],
        scratch_shapes=[pltpu.VMEM((3, d, COL), BF16)],
        compiler_params=_params(1),
        name="norm_proj_undilated",
    )(x, g.reshape(1, d), w_in, w_in, w_in)


def proj_lattice(h, w_in, layer, group, r, tm=1024):
    s, d = h.shape
    assert r in (LATTICE_STRIDE, LATTICE_STRIDE ** 2)
    n_slabs = 3 * COL // LANES
    two_pass_rows = tm // LATTICE_STRIDE if r == LATTICE_STRIDE ** 2 else SUBLANES

    def w_spec(j):
        return pl.BlockSpec((None, d, COL), lambda m: (layer, 0, QC_BLOCK0 + group + 3 * j),
                            pipeline_mode=pl.Buffered(1))

    return pl.pallas_call(
        functools.partial(_lattice_kernel, r=r),
        grid=(s // tm,),
        in_specs=[pl.BlockSpec((tm, d), lambda m: (m, 0)), w_spec(0), w_spec(1), w_spec(2)],
        out_specs=pl.BlockSpec((r, tm // r, 3 * COL), lambda m: (0, m, 0)),
        out_shape=jax.ShapeDtypeStruct((r, s // r, 3 * COL), BF16),
        scratch_shapes=[pltpu.VMEM((3, d, COL), BF16),
                        pltpu.VMEM((n_slabs, tm, LANES), F32),
                        pltpu.VMEM((n_slabs, LATTICE_STRIDE, two_pass_rows, LANES), F32)],
        compiler_params=_params(1),
        name=f"proj_lattice_r{r}",
    )(h, w_in, w_in, w_in)


def _merge_kernel(h_ref, ca_ref, ob_ref, oc_ref, wg0_ref, wg1_ref, wg2_ref,
                  wa_ref, wb_ref, wc_ref, o_ref, wa16_ref, wb16_ref, wc16_ref):
    @pl.when(pl.program_id(1) == 0)
    def _():
        _cast_rows(wa_ref, wa16_ref)
        _cast_rows(wb_ref, wb16_ref)
        _cast_rows(wc_ref, wc16_ref)

    h = h_ref[...]
    ys = (ca_ref[...], ob_ref[...], oc_ref[...])
    ws = (wa16_ref, wb16_ref, wc16_ref)
    wgs = (wg0_ref, wg1_ref, wg2_ref)
    for c0 in range(0, o_ref.shape[1], MXU_COLS):
        cols = slice(c0, c0 + MXU_COLS)
        mixed = None
        for b in range(N_BRANCHES):
            gate = jax.nn.sigmoid(jnp.dot(h, wgs[b][:, cols], preferred_element_type=F32))
            term = gate * jnp.dot(ys[b], ws[b][:, cols], preferred_element_type=F32)
            mixed = term if mixed is None else mixed + term
        o_ref[:, cols] = mixed.astype(o_ref.dtype)


def gated_merge(h, ca, ob, oc, gates16, w_a, w_b, w_c, layer, tm=1024, tn=COL):
    s, d = h.shape
    nj = d // tn
    widths = (ca.shape[1], ob.shape[1], oc.shape[1])

    def gate_spec(b):
        return pl.BlockSpec((d, tn), lambda n, m: (0, b * nj + n))

    def lhs_spec(width):
        return pl.BlockSpec((tm, width), lambda n, m: (m, 0))

    def w_spec(width):
        return pl.BlockSpec((None, width, tn), lambda n, m: (layer, 0, n),
                            pipeline_mode=pl.Buffered(1))

    return pl.pallas_call(
        _merge_kernel,
        grid=(nj, s // tm),
        in_specs=[lhs_spec(d)] + [lhs_spec(w) for w in widths]
        + [gate_spec(0), gate_spec(1), gate_spec(2)] + [w_spec(w) for w in widths],
        out_specs=pl.BlockSpec((tm, tn), lambda n, m: (m, n)),
        out_shape=jax.ShapeDtypeStruct((s, d), BF16),
        scratch_shapes=[pltpu.VMEM((w, tn), BF16) for w in widths],
        compiler_params=_params(2),
        name="gated_merge",
    )(h, ca, ob, oc, gates16, gates16, gates16, w_a, w_b, w_c)


def _out_proj_kernel(a_ref, w_ref, x_ref, g_ref, xo_ref, ho_ref, w16_ref):
    @pl.when(pl.program_id(0) == 0)
    def _():
        _cast_rows(w_ref, w16_ref)

    half = a_ref.shape[0] // 2
    for rows in (slice(0, half), slice(half, 2 * half)):
        x = x_ref[rows, :] + jnp.dot(a_ref[rows, :], w16_ref[...], preferred_element_type=F32)
        xo_ref[rows, :] = x
        ho_ref[rows, :] = _rms(x, g_ref[...]).astype(ho_ref.dtype)


def out_proj_norm(a, w_o, layer, x, g, tm=512):
    s, k = a.shape
    d = x.shape[1]
    row = pl.BlockSpec((tm, d), lambda m: (m, 0))
    return pl.pallas_call(
        _out_proj_kernel,
        grid=(s // tm,),
        in_specs=[pl.BlockSpec((tm, k), lambda m: (m, 0)),
                  pl.BlockSpec((None, k, d), lambda m: (layer, 0, 0), pipeline_mode=pl.Buffered(1)),
                  row,
                  pl.BlockSpec((1, d), lambda m: (0, 0))],
        out_specs=[row, row],
        out_shape=[jax.ShapeDtypeStruct((s, d), F32), jax.ShapeDtypeStruct((s, d), BF16)],
        scratch_shapes=[pltpu.VMEM((k, d), BF16)],
        compiler_params=_params(1),
        name="out_proj_norm",
    )(a, w_o, x, g.reshape(1, d))


def _swiglu_kernel(h_ref, wg_ref, wu_ref, o_ref, w16_ref):
    @pl.when(pl.program_id(1) == 0)
    def _():
        _cast_rows(wg_ref, w16_ref.at[0])
        _cast_rows(wu_ref, w16_ref.at[1])

    h = h_ref[...]
    for c0 in range(0, o_ref.shape[1], MXU_COLS):
        cols = slice(c0, c0 + MXU_COLS)
        g = jnp.dot(h, w16_ref[0, :, cols], preferred_element_type=F32)
        u = jnp.dot(h, w16_ref[1, :, cols], preferred_element_type=F32)
        o_ref[:, cols] = (jax.nn.silu(g) * u).astype(o_ref.dtype)


def swiglu_in(h, w_ffn_in, layer, tm=2048, tn=COL):
    s, d = h.shape
    nj = D_FF // tn
    return pl.pallas_call(
        _swiglu_kernel,
        grid=(nj, s // tm),
        in_specs=[pl.BlockSpec((tm, d), lambda n, m: (m, 0)),
                  pl.BlockSpec((None, d, tn), lambda n, m: (layer, 0, n)),
                  pl.BlockSpec((None, d, tn), lambda n, m: (layer, 0, nj + n))],
        out_specs=pl.BlockSpec((tm, tn), lambda n, m: (m, n)),
        out_shape=jax.ShapeDtypeStruct((s, D_FF), BF16),
        scratch_shapes=[pltpu.VMEM((2, d, tn), BF16)],
        compiler_params=_params(2),
        name="swiglu_in",
    )(h, w_ffn_in, w_ffn_in)


def _mm_residual_kernel(a_ref, w_ref, x_ref, o_ref):
    o_ref[...] = x_ref[...] + jnp.dot(a_ref[...], w_ref[...], preferred_element_type=F32)


def matmul_residual(a, w16, x, tm=1024, tn=COL):
    s, k = a.shape
    d = w16.shape[1]
    return pl.pallas_call(
        _mm_residual_kernel,
        grid=(s // tm, d // tn),
        in_specs=[pl.BlockSpec((tm, k), lambda i, j: (i, 0)),
                  pl.BlockSpec((k, tn), lambda i, j: (0, j)),
                  pl.BlockSpec((tm, tn), lambda i, j: (i, j))],
        out_specs=pl.BlockSpec((tm, tn), lambda i, j: (i, j)),
        out_shape=jax.ShapeDtypeStruct((s, d), F32),
        compiler_params=_params(2),
        name="matmul_residual",
    )(a, w16, x)


def _mm_residual_norm_kernel(a_ref, w_ref, x_ref, g_ref, o_ref, ss_ref):
    n = pl.program_id(1)
    tn = x_ref.shape[1]
    x = x_ref[...] + jnp.dot(a_ref[...], w_ref[...], preferred_element_type=F32)
    ss = jnp.sum(x * x, axis=-1, keepdims=True)

    @pl.when(n == 0)
    def _():
        ss_ref[...] = jnp.broadcast_to(ss, ss_ref.shape)

    @pl.when(n > 0)
    def _():
        ss_ref[...] += ss

    for j in range(o_ref.shape[1] // tn):
        @pl.when(n == j)
        def _():
            o_ref[:, j * tn:(j + 1) * tn] = x

    @pl.when(n == pl.num_programs(1) - 1)
    def _():
        ms = ss_ref[:, 0:1] * (1.0 / o_ref.shape[1])
        o_ref[...] = o_ref[...] * lax.rsqrt(ms + EPS) * g_ref[...]


def matmul_residual_norm(a, w16, x, g, tm=1024, tn=COL):
    s, k = a.shape
    d = w16.shape[1]
    return pl.pallas_call(
        _mm_residual_norm_kernel,
        grid=(s // tm, d // tn),
        in_specs=[pl.BlockSpec((tm, k), lambda i, j: (i, 0)),
                  pl.BlockSpec((k, tn), lambda i, j: (0, j)),
                  pl.BlockSpec((tm, tn), lambda i, j: (i, j)),
                  pl.BlockSpec((1, d), lambda i, j: (0, 0))],
        out_specs=pl.BlockSpec((tm, d), lambda i, j: (i, 0)),
        out_shape=jax.ShapeDtypeStruct((s, d), F32),
        scratch_shapes=[pltpu.VMEM((tm, LANES), F32)],
        compiler_params=_params(2),
        name="matmul_residual_norm",
    )(a, w16, x, g.reshape(1, d))


def _conv_tile(up_ref, uc_ref, un_ref, w_ref, b_ref, g_ref, beta_ref, o_ref, z_ref, sh_ref,
               *, first, last, chunk=64):
    tm, halo = CONV_TILE, CONV_HALO

    def glu(u):
        u = u.astype(F32)
        return u[:, :CONV_CH] * jax.nn.sigmoid(u[:, CONV_CH:])

    z_ref[0:halo] = jnp.where(first, 0.0, glu(up_ref[...]))
    z_ref[halo:halo + tm] = glu(uc_ref[...])
    z_ref[halo + tm:2 * halo + tm] = jnp.where(last, 0.0, glu(un_ref[...]))
    span = sh_ref.shape[1]
    for j in range(1, SUBLANES):
        sh_ref[j - 1] = z_ref[j:j + span, :]

    for rc in range(tm // chunk):
        acc = jnp.zeros((chunk, CONV_CH), F32)
        for k in range(CONV_WIDTH):
            off = rc * chunk + halo - CONV_HALF + k
            j = off % SUBLANES
            src = z_ref if j == 0 else sh_ref.at[j - 1]
            acc = acc + src[off - j:off - j + chunk, :] * w_ref[k:k + 1, :]
        z = acc + b_ref[...]
        mu = jnp.mean(z, axis=-1, keepdims=True)
        zc = z - mu
        var = jnp.mean(zc * zc, axis=-1, keepdims=True)
        y = zc * lax.rsqrt(var + EPS) * g_ref[...] + beta_ref[...]
        o_ref[rc * chunk:(rc + 1) * chunk, :] = jax.nn.silu(y).astype(o_ref.dtype)


def _conv_kernel(up_ref, uc_ref, un_ref, w_ref, b_ref, g_ref, beta_ref, wo_ref,
                 o_ref, wo16_ref, z_ref, sh_ref):
    i = pl.program_id(0)
    _conv_tile(up_ref, uc_ref, un_ref, w_ref, b_ref, g_ref, beta_ref, o_ref, z_ref, sh_ref,
               first=i == 0, last=i == pl.num_programs(0) - 1)
    wo16_ref[...] = wo_ref[...].astype(wo16_ref.dtype)


def conv_module(plain, conv_w, conv_b, cnorm_g, cnorm_b, w_ffn_out, layer):
    s = plain.shape[0]
    steps = s // CONV_TILE
    nh = CONV_TILE // CONV_HALO
    width = 2 * CONV_CH
    _, d_ff, d_out = w_ffn_out.shape
    wo_rows = d_ff // steps
    assert wo_rows * steps == d_ff and wo_rows % BF16_ROWS == 0
    vec_spec = pl.BlockSpec((None, 1, CONV_CH), lambda i: (layer, 0, 0))
    row = lambda v: v.reshape(v.shape[0], 1, CONV_CH)
    return pl.pallas_call(
        _conv_kernel,
        grid=(steps,),
        in_specs=[pl.BlockSpec((CONV_HALO, width), lambda i: (jnp.maximum(i * nh - 1, 0), 0)),
                  pl.BlockSpec((CONV_TILE, width), lambda i: (i, 0)),
                  pl.BlockSpec((CONV_HALO, width),
                               lambda i: (jnp.minimum((i + 1) * nh, s // CONV_HALO - 1), 0)),
                  pl.BlockSpec((None, CONV_WIDTH, CONV_CH), lambda i: (layer, 0, 0)),
                  vec_spec, vec_spec, vec_spec,
                  pl.BlockSpec((None, wo_rows, d_out), lambda i: (layer, i, 0))],
        out_specs=[pl.BlockSpec((CONV_TILE, CONV_CH), lambda i: (i, 0)),
                   pl.BlockSpec((wo_rows, d_out), lambda i: (i, 0))],
        out_shape=[jax.ShapeDtypeStruct((s, CONV_CH), BF16),
                   jax.ShapeDtypeStruct((d_ff, d_out), BF16)],
        scratch_shapes=[pltpu.VMEM((CONV_TILE + 2 * CONV_HALO, CONV_CH), F32),
                        pltpu.VMEM((SUBLANES - 1, CONV_TILE + 2 * CONV_HALO - SUBLANES, CONV_CH), F32)],
        compiler_params=_params(1),
        name="conv_module",
    )(plain, plain, plain, conv_w, row(conv_b), row(cnorm_g), row(cnorm_b), w_ffn_out)


def _band_tables(bq, half, step):
    shape = (bq, bq + 2 * half)
    qi = lax.broadcasted_iota(jnp.int32, shape, 0)
    kj = lax.broadcasted_iota(jnp.int32, shape, 1)
    rel = jnp.abs(kj - half - qi)
    return rel <= half, (step * rel).astype(F32), kj


def _edge_valid(band, kj, bq, half, blk, n_blocks, first, last):
    valid = band
    if first:
        valid = valid & (kj >= jnp.where(blk > 0, 0, half))
    if last:
        valid = valid & (kj < jnp.where(blk < n_blocks - 1, bq + 2 * half, bq + half))
    return valid


def _scores(q, k):
    return lax.dot_general(q, k, (((1,), (1,)), ((), ())), preferred_element_type=F32)


def _softmax_rows(sc, bias2, valid, sink2):
    t = jnp.where(valid, sc * (HEAD_DIM ** -0.5 * LOG2E) + bias2, NEG * LOG2E)
    m2 = jnp.max(t, axis=-1, keepdims=True)
    if sink2 is not None:
        m2 = jnp.maximum(m2, sink2)
    p = jnp.exp2(t - m2)
    denom = jnp.sum(p, axis=-1, keepdims=True)
    if sink2 is not None:
        denom = denom + jnp.exp2(sink2 - m2)
    return p, m2, denom


def _swa_kernel(slopes_ref, sink_ref, q_ref, kp_ref, kc_ref, kn_ref, vp_ref, vc_ref, vn_ref, wg_ref,
                o_ref, wg16_ref, *, tq, half, n_blocks, layer):
    i = pl.program_id(0)
    hk = pl.program_id(1)

    @pl.when(i * pl.num_programs(1) + hk < N_GATE_BLOCKS)
    def _():
        _cast_rows(wg_ref, wg16_ref)

    bq = half
    nsub = tq // bq
    kcat = jnp.concatenate([kp_ref[...], kc_ref[...], kn_ref[...]], axis=0)
    vcat = jnp.concatenate([vp_ref[...], vc_ref[...], vn_ref[...]], axis=0)
    band, dist, kj = _band_tables(bq, half, 1)
    heads = [hk * B_GROUP + g for g in range(B_GROUP)]
    biases = [(-LOG2E * slopes_ref[hd]) * dist for hd in heads]
    sinks = [LOG2E * sink_ref[layer, hd] for hd in heads]
    for s in range(nsub):
        rows = slice(s * bq, (s + 1) * bq)
        win = slice(s * bq, (s + 1) * bq + 2 * half)
        valid = _edge_valid(band, kj, bq, half, i * nsub + s, n_blocks, s == 0, s == nsub - 1)
        q_all = jnp.concatenate(
            [q_ref[rows, g * HEAD_DIM:(g + 1) * HEAD_DIM] for g in range(B_GROUP)], axis=0)
        sc_all = _scores(q_all, kcat[win])
        ps, denoms = [], []
        for g in range(B_GROUP):
            p, _, denom = _softmax_rows(sc_all[g * bq:(g + 1) * bq], biases[g], valid, sinks[g])
            ps.append(p.astype(BF16))
            denoms.append(denom)
        o_all = jnp.dot(jnp.concatenate(ps, axis=0), vcat[win], preferred_element_type=F32)
        for g in range(B_GROUP):
            o = o_all[g * bq:(g + 1) * bq] / denoms[g]
            o_ref[rows, g * HEAD_DIM:(g + 1) * HEAD_DIM] = o.astype(o_ref.dtype)


def windowed_gqa(plain, slopes, sink, w_in, layer, tq=1024):
    s = plain.shape[0]
    d = w_in.shape[1]
    assert (s // tq) * B_KV_HEADS >= N_GATE_BLOCKS
    gate_blk = lambda i, h: jnp.minimum(i * B_KV_HEADS + h, N_GATE_BLOCKS - 1)
    half = B_HALF_WINDOW
    nh = tq // half
    qw = B_GROUP * HEAD_DIM
    q_blk0 = 2 * CONV_CH // qw
    k_blk0 = (2 * CONV_CH + B_Q_HEADS * HEAD_DIM) // HEAD_DIM
    v_blk0 = k_blk0 + B_KV_HEADS

    def halo_specs(blk0):
        return [pl.BlockSpec((half, HEAD_DIM), lambda i, h: (jnp.maximum(i * nh - 1, 0), blk0 + h)),
                pl.BlockSpec((tq, HEAD_DIM), lambda i, h: (i, blk0 + h)),
                pl.BlockSpec((half, HEAD_DIM),
                             lambda i, h: (jnp.minimum((i + 1) * nh, s // half - 1), blk0 + h))]

    smem = pl.BlockSpec(memory_space=pltpu.SMEM)
    return pl.pallas_call(
        functools.partial(_swa_kernel, tq=tq, half=half, n_blocks=s // half, layer=layer),
        grid=(s // tq, B_KV_HEADS),
        in_specs=[smem, smem, pl.BlockSpec((tq, qw), lambda i, h: (i, q_blk0 + h))]
        + halo_specs(k_blk0) + halo_specs(v_blk0)
        + [pl.BlockSpec((None, d, COL), lambda i, h: (layer, 0, GATE_BLOCK0 + gate_blk(i, h)))],
        out_specs=[pl.BlockSpec((tq, qw), lambda i, h: (i, h)),
                   pl.BlockSpec((d, COL), lambda i, h: (0, gate_blk(i, h)))],
        out_shape=[jax.ShapeDtypeStruct((s, B_Q_HEADS * HEAD_DIM), BF16),
                   jax.ShapeDtypeStruct((d, N_GATE_BLOCKS * COL), BF16)],
        compiler_params=_params(2),
        name="windowed_gqa",
    )(slopes, sink, plain, plain, plain, plain, plain, plain, plain, w_in)


def _dilated_kernel(slopes_ref, *refs, tokens, bq, half, seq_len):
    o_ref, u_scr, m_scr, d_scr = refs[-4:]
    i = pl.program_id(0)
    h = pl.program_id(1)
    for gi, (_, r) in enumerate(C_PATTERNS):
        q_ref, kp_ref, kc_ref, kn_ref, vp_ref, vc_ref, vn_ref = refs[7 * gi:7 * gi + 7]
        rows = tokens // r
        nsub = rows // bq
        n_blocks = seq_len // r // bq
        band, dist, kj = _band_tables(bq, half, r)
        bias = (-LOG2E * slopes_ref[B_Q_HEADS + gi * C_HEADS_PER_GROUP + h]) * dist
        for c in range(r):
            kcat = jnp.concatenate([kp_ref[c], kc_ref[c], kn_ref[c]], axis=0)
            vcat = jnp.concatenate([vp_ref[c], vc_ref[c], vn_ref[c]], axis=0)
            for s in range(nsub):
                win = slice(s * bq, (s + 1) * bq + 2 * half)
                valid = _edge_valid(band, kj, bq, half, i * nsub + s, n_blocks, s == 0, s == nsub - 1)
                p, m2, denom = _softmax_rows(_scores(q_ref[c, s * bq:(s + 1) * bq, :], kcat[win]),
                                             bias, valid, None)
                dst = pl.ds(s * bq * r + c, bq, stride=r) if r > 1 else pl.ds(s * bq, bq)
                u_scr[gi, dst, :] = jnp.dot(p.astype(BF16), vcat[win], preferred_element_type=F32)
                m_scr[gi, dst, :] = jnp.broadcast_to(m2, (bq, HEAD_DIM))
                d_scr[gi, dst, :] = jnp.broadcast_to(denom, (bq, HEAD_DIM))
    def combine(ci, carry):
        rows = pl.ds(pl.multiple_of(ci * bq, bq), bq)
        ms = [m_scr[gi, rows, :] for gi in range(len(C_PATTERNS))]
        m_max = jnp.maximum(jnp.maximum(ms[0], ms[1]), ms[2])
        num = den = None
        for gi in range(len(C_PATTERNS)):
            w = jnp.exp2(ms[gi] - m_max)
            num = w * u_scr[gi, rows, :] if num is None else num + w * u_scr[gi, rows, :]
            den = w * d_scr[gi, rows, :] if den is None else den + w * d_scr[gi, rows, :]
        o_ref[rows, :] = (num / den).astype(o_ref.dtype)
        return carry

    lax.fori_loop(0, tokens // bq, combine, 0)


def dilated_attention(lattices, slopes, seq_len, tokens=C_TOKENS, bq=C_BLOCK_Q, half=C_HALF):
    nq = C_HEADS_PER_GROUP
    in_specs = [pl.BlockSpec(memory_space=pltpu.SMEM)]
    args = [slopes]
    for (_, r), lat in zip(C_PATTERNS, lattices):
        rows = tokens // r
        nh = rows // half
        last = seq_len // r // half - 1

        def specs(blk0, r=r, rows=rows, nh=nh, last=last):
            return [pl.BlockSpec((r, half, HEAD_DIM), lambda i, h: (0, jnp.maximum(i * nh - 1, 0), blk0 + h)),
                    pl.BlockSpec((r, rows, HEAD_DIM), lambda i, h: (0, i, blk0 + h)),
                    pl.BlockSpec((r, half, HEAD_DIM), lambda i, h: (0, jnp.minimum((i + 1) * nh, last), blk0 + h))]

        in_specs += [pl.BlockSpec((r, rows, HEAD_DIM), lambda i, h: (0, i, h))] + specs(nq) + specs(2 * nq)
        args += [lat] * 7
    return pl.pallas_call(
        functools.partial(_dilated_kernel, tokens=tokens, bq=bq, half=half, seq_len=seq_len),
        grid=(seq_len // tokens, nq),
        in_specs=in_specs,
        out_specs=pl.BlockSpec((tokens, HEAD_DIM), lambda i, h: (i, h)),
        out_shape=jax.ShapeDtypeStruct((seq_len, nq * HEAD_DIM), BF16),
        scratch_shapes=[pltpu.VMEM((len(C_PATTERNS), tokens, HEAD_DIM), F32)] * 3,
        compiler_params=_params(2),
        name="dilated_attention",
    )(*args)


def _alibi_slopes():
    h = jnp.arange(1, N_ATTN_HEADS + 1, dtype=F32)
    return jnp.exp2(-ALIBI_MAX_EXP * h / N_ATTN_HEADS)


def kernel(x, ln1_g, w_in, conv_w, conv_b, cnorm_g, cnorm_b, w_a, sink, w_b, w_c, w_o, ln2_g,
           w_ffn_in, w_ffn_out, lnf_g):
    b, s, d = x.shape
    depth = ln1_g.shape[0]
    assert depth >= 1
    slopes = _alibi_slopes()
    outs = []
    for xb in jnp.split(x, b, axis=0):
        xb = xb.reshape(s, d)
        for l in range(depth):
            lat0, h = norm_proj_undilated(xb, ln1_g[l], w_in, l)
            plain = proj_plain(h, w_in, l)
            lattices = [lat0] + [proj_lattice(h, w_in, l, gi, r)
                                 for gi, (_, r) in enumerate(C_PATTERNS) if gi > 0]
            ca, w_out16 = conv_module(plain, conv_w, conv_b, cnorm_g, cnorm_b, w_ffn_out, l)
            ob, gates16 = windowed_gqa(plain, slopes, sink, w_in, l)
            oc = dilated_attention(lattices, slopes, s)
            mixed = gated_merge(h, ca, ob, oc, gates16, w_a, w_b, w_c, l)
            xb, h2 = out_proj_norm(mixed, w_o, l, xb, ln2_g[l])
            act = swiglu_in(h2, w_ffn_in, l)
            if l < depth - 1:
                xb = matmul_residual(act, w_out16, xb)
            else:
                xb = matmul_residual_norm(act, w_out16, xb, lnf_g)
        outs.append(xb.reshape(1, s, d))
    return outs[0] if b == 1 else jnp.concatenate(outs, axis=0)
```
